```python
import math
import jax, jax.numpy as jnp
from jax import lax
import numpy as np

D_MODEL = 1024
BATCH = 16
SEQ = 4096
DEPTH = 4

EXPAND = 2
D_INNER = EXPAND * D_MODEL
CONV_WIDTH = 31
CHUNK = 128
SGU_GROUPS = 8
PLE_DIM = 256
N_MIXERS = 2
N_CONV_LAYERS = (DEPTH + 1) // 2
N_SGU_LAYERS = DEPTH // 2
EPS = 1e-6

kernel_name = "hybrid_conformer_conv_gmlp_trunk"


def _rmsnorm(x, g):
    xf = x.astype(jnp.float32)
    y = xf * lax.rsqrt(jnp.mean(xf * xf, axis=-1, keepdims=True) + EPS)
    return (y * g.astype(jnp.float32)).astype(x.dtype)


def _layernorm(x, g, b):
    xf = x.astype(jnp.float32)
    mu = jnp.mean(xf, axis=-1, keepdims=True)
    xc = xf - mu
    var = jnp.mean(xc * xc, axis=-1, keepdims=True)
    y = xc * lax.rsqrt(var + EPS)
    return (y * g.astype(jnp.float32) + b.astype(jnp.float32)).astype(x.dtype)


def _causal_depthwise_conv(x, w, b):
    k = w.shape[0]
    y = lax.conv_general_dilated(
        x, w[:, None, :].astype(x.dtype),
        window_strides=(1,), padding=[(k - 1, 0)],
        dimension_numbers=("NWC", "WIO", "NWC"),
        feature_group_count=x.shape[-1])
    return y + b.astype(x.dtype)


def _conformer_conv_mixer(a, b_gate, conv_w, conv_b, ln_g, ln_b):
    y = a * jax.nn.sigmoid(b_gate)
    y = _causal_depthwise_conv(y, conv_w, conv_b)
    y = _layernorm(y, ln_g, ln_b)
    return jax.nn.silu(y)


def _chunked_sgu_mixer(a, b_half, ln_g, ln_b, w_s, b_s):
    bsz, seq, e = a.shape
    n_chunks = seq // CHUNK
    u = jax.nn.gelu(a, approximate=False)
    v = _layernorm(jax.nn.gelu(b_half, approximate=False), ln_g, ln_b)
    vg = v.reshape(bsz, n_chunks, CHUNK, SGU_GROUPS, e // SGU_GROUPS)
    mask = jnp.tril(jnp.ones((CHUNK, CHUNK), dtype=bool))
    w = jnp.where(mask[None], w_s, jnp.zeros((), w_s.dtype)).astype(v.dtype)
    mixed = jnp.einsum("gts,bnsgc->bntgc", w, vg)
    mixed = mixed + b_s.T.astype(v.dtype)[None, None, :, :, None]
    return u * mixed.reshape(bsz, seq, e)


def setup_inputs(seed: int = 0) -> dict:
    key = jax.random.key(seed)
    ks = jax.random.split(key, 20)
    f32 = jnp.float32
    E = D_INNER
    x = jax.random.normal(ks[0], (BATCH, SEQ, D_MODEL), f32)
    p = jax.random.normal(ks[1], (DEPTH, BATCH, SEQ, PLE_DIM), f32)
    norm_g = 1.0 + 0.02 * jax.random.normal(ks[2], (DEPTH, D_MODEL), f32)
    w_in = jax.random.normal(ks[3], (DEPTH, D_MODEL, 3 * E), f32) * D_MODEL ** -0.5
    w_out = jax.random.normal(ks[4], (DEPTH, E, D_MODEL), f32) * E ** -0.5
    conv_w = jax.random.normal(ks[5], (N_CONV_LAYERS, CONV_WIDTH, E), f32) * CONV_WIDTH ** -0.5
    conv_b = 0.02 * jax.random.normal(ks[6], (N_CONV_LAYERS, E), f32)
    conv_ln_g = 1.0 + 0.02 * jax.random.normal(ks[7], (N_CONV_LAYERS, E), f32)
    conv_ln_b = 0.02 * jax.random.normal(ks[8], (N_CONV_LAYERS, E), f32)
    sgu_ln_g = 1.0 + 0.02 * jax.random.normal(ks[9], (N_SGU_LAYERS, E), f32)
    sgu_ln_b = 0.02 * jax.random.normal(ks[10], (N_SGU_LAYERS, E), f32)
    sgu_w = jax.random.normal(ks[11], (N_SGU_LAYERS, SGU_GROUPS, CHUNK, CHUNK), f32) * CHUNK ** -0.5
    sgu_b = 1.0 + 0.1 * jax.random.normal(ks[12], (N_SGU_LAYERS, SGU_GROUPS, CHUNK), f32)
    pl_norm_g = 1.0 + 0.02 * jax.random.normal(ks[13], (DEPTH, D_MODEL), f32)
    pl_gate_w = jax.random.normal(ks[14], (DEPTH, D_MODEL, D_MODEL), f32) * D_MODEL ** -0.5
    pl_proj_w = jax.random.normal(ks[15], (DEPTH, PLE_DIM, D_MODEL), f32) * PLE_DIM ** -0.5
    final_g = 1.0 + 0.02 * jax.random.normal(ks[16], (D_MODEL,), f32)
    return {"x": x, "p": p, "norm_g": norm_g, "w_in": w_in, "w_out": w_out,
            "conv_w": conv_w, "conv_b": conv_b, "conv_ln_g": conv_ln_g, "conv_ln_b": conv_ln_b,
            "sgu_ln_g": sgu_ln_g, "sgu_ln_b": sgu_ln_b, "sgu_w": sgu_w, "sgu_b": sgu_b,
            "pl_norm_g": pl_norm_g, "pl_gate_w": pl_gate_w, "pl_proj_w": pl_proj_w,
            "final_g": final_g}


def reference(x, p, norm_g, w_in, w_out, conv_w, conv_b, conv_ln_g, conv_ln_b,
              sgu_ln_g, sgu_ln_b, sgu_w, sgu_b, pl_norm_g, pl_gate_w, pl_proj_w, final_g):
    for i in range(DEPTH):
        h = _rmsnorm(x, norm_g[i])
        proj = jnp.einsum("bsd,de->bse", h, w_in[i])
        a, b_half, z = jnp.split(proj, 3, axis=-1)
        j = i // N_MIXERS
        if i % N_MIXERS == 0:
            y = _conformer_conv_mixer(a, b_half, conv_w[j], conv_b[j], conv_ln_g[j], conv_ln_b[j])
        else:
            y = _chunked_sgu_mixer(a, b_half, sgu_ln_g[j], sgu_ln_b[j], sgu_w[j], sgu_b[j])
        x = x + jnp.einsum("bse,ed->bsd", y * jax.nn.silu(z), w_out[i])
        gate = jax.nn.sigmoid(jnp.einsum("bsd,de->bse", _rmsnorm(x, pl_norm_g[i]), pl_gate_w[i]))
        x = x + gate * jnp.einsum("bsk,kd->bsd", p[i], pl_proj_w[i])
    return _rmsnorm(x, final_g)
```

```python
import functools

import jax
import jax.numpy as jnp
from jax import lax
from jax.experimental import pallas as pl
from jax.experimental.pallas import tpu as pltpu

F32 = jnp.float32
BF16 = jnp.bfloat16

EPS = 1e-6
CONV_K = 31
SGU_CHUNK = 128
LANES = 128
TM = 512
HALO = 32
CN = 256
RB = 32
VMEM_LIMIT_BYTES = 56 * 1024 * 1024


def _dot(a, b):
    return jnp.dot(a, b, preferred_element_type=F32)


def _rms(x, g):
    ms = jnp.mean(x * x, axis=-1, keepdims=True)
    return x * lax.rsqrt(ms + EPS) * g


def _silu(x):
    return x * jax.nn.sigmoid(x)


def _gelu(x):
    return 0.5 * x * (1.0 + lax.erf(x * (2.0 ** -0.5)))


def _norm_rows(x_ref, g_ref, h_ref):
    rows = 64

    def body(i, _):
        r = pl.multiple_of(i * rows, rows)
        xb = x_ref[pl.ds(r, rows), :]
        h_ref[pl.ds(r, rows), :] = _rms(xb, g_ref[...]).astype(BF16)
        return 0

    lax.fori_loop(0, TM // rows, body, 0)


def _finish_stats(s1_ref, s2_ref, mu_ref, rstd_ref, width):
    mu = jnp.sum(s1_ref[...], axis=-1, keepdims=True) * (1.0 / width)
    ex2 = jnp.sum(s2_ref[...], axis=-1, keepdims=True) * (1.0 / width)
    var = ex2 - mu * mu
    mu_ref[...] = jnp.broadcast_to(mu, (TM, LANES))
    rstd_ref[...] = jnp.broadcast_to(lax.rsqrt(var + EPS), (TM, LANES))


def _tail(x_ref, p_ref, y_ref, wout_ref, png_ref, gw_ref, pw_ref, fg_ref, o_ref, final):
    xo = x_ref[...] + _dot(y_ref[...], wout_ref[...])
    r = _rms(xo, png_ref[...]).astype(BF16)
    gate = jax.nn.sigmoid(_dot(r, gw_ref[...]))
    pp = _dot(p_ref[...].astype(BF16), pw_ref[...])
    xo = xo + gate * pp
    if final:
        xo = _rms(xo, fg_ref[...])
    o_ref[...] = xo


def _conv_layer_kernel(x_ref, p_ref, ng_ref, win_ref, wout_ref, cw_ref, cb_ref,
                       lng_ref, lnb_ref, png_ref, gw_ref, pw_ref, fg_ref, o_ref,
                       h_ref, g_ref, c_ref, y_ref, s1_ref, s2_ref, mu_ref, rstd_ref,
                       *, tiles_per_seq, final):
    e = y_ref.shape[1]
    n_chunks = e // CN
    n_blocks = e // LANES
    per_chunk = CN // LANES

    first = pl.program_id(0) % tiles_per_seq == 0

    @pl.when(first)
    def _():
        g_ref[:, 0:HALO, :] = jnp.zeros((n_blocks, HALO, LANES), F32)

    @pl.when(jnp.logical_not(first))
    def _():
        g_ref[:, 0:HALO, :] = g_ref[:, TM:TM + HALO, :]

    _norm_rows(x_ref, ng_ref, h_ref)

    def glu_chunk(j, _):
        h = h_ref[...]
        a = _dot(h, win_ref[0, j])
        b = _dot(h, win_ref[1, j])
        g = a * jax.nn.sigmoid(b)
        for l in range(per_chunk):
            g_ref[j * per_chunk + l, HALO:HALO + TM, :] = g[:, l * LANES:(l + 1) * LANES]
        return 0

    lax.fori_loop(0, n_chunks, glu_chunk, 0)

    s1_ref[...] = jnp.zeros((TM, LANES), F32)
    s2_ref[...] = jnp.zeros((TM, LANES), F32)

    def conv_block(cb, _):
        w = cw_ref[cb]
        bias = cb_ref[cb]
        for rb in range(TM // RB):
            r0 = rb * RB
            acc = jnp.broadcast_to(bias, (RB, LANES))
            for k in range(CONV_K):
                s = r0 + HALO - (CONV_K - 1) + k
                acc = acc + w[k:k + 1, :] * g_ref[cb, s:s + RB, :]
            c_ref[cb, r0:r0 + RB, :] = acc
            s1_ref[r0:r0 + RB, :] += acc
            s2_ref[r0:r0 + RB, :] += acc * acc
        return 0

    lax.fori_loop(0, n_blocks, conv_block, 0)

    _finish_stats(s1_ref, s2_ref, mu_ref, rstd_ref, e)

    def out_chunk(j, _):
        z = _dot(h_ref[...], win_ref[2, j])
        mu = mu_ref[...]
        rstd = rstd_ref[...]
        lng = lng_ref[j]
        lnb = lnb_ref[j]
        for l in range(per_chunk):
            sl = slice(l * LANES, (l + 1) * LANES)
            yn = (c_ref[j * per_chunk + l] - mu) * rstd * lng[:, sl] + lnb[:, sl]
            y = _silu(yn) * _silu(z[:, sl])
            col = pl.multiple_of(j * CN + l * LANES, LANES)
            y_ref[:, pl.ds(col, LANES)] = y.astype(BF16)
        return 0

    lax.fori_loop(0, n_chunks, out_chunk, 0)

    _tail(x_ref, p_ref, y_ref, wout_ref, png_ref, gw_ref, pw_ref, fg_ref, o_ref, final)


def _sgu_layer_kernel(x_ref, p_ref, ng_ref, win_ref, wout_ref, sw_ref, sb_ref,
                      lng_ref, lnb_ref, png_ref, gw_ref, pw_ref, fg_ref, o_ref,
                      h_ref, u_ref, c_ref, y_ref, s1_ref, s2_ref, mu_ref, rstd_ref,
                      *, final):
    e = y_ref.shape[1]
    n_chunks = e // CN
    per_chunk = CN // LANES

    _norm_rows(x_ref, ng_ref, h_ref)

    s1_ref[...] = jnp.zeros((TM, LANES), F32)
    s2_ref[...] = jnp.zeros((TM, LANES), F32)

    def gelu_chunk(j, _):
        h = h_ref[...]
        u_ref[j] = _gelu(_dot(h, win_ref[0, j]))
        gb = _gelu(_dot(h, win_ref[1, j]))
        c_ref[j] = gb
        s1 = s1_ref[...]
        s2 = s2_ref[...]
        for l in range(per_chunk):
            blk = gb[:, l * LANES:(l + 1) * LANES]
            s1 = s1 + blk
            s2 = s2 + blk * blk
        s1_ref[...] = s1
        s2_ref[...] = s2
        return 0

    lax.fori_loop(0, n_chunks, gelu_chunk, 0)

    _finish_stats(s1_ref, s2_ref, mu_ref, rstd_ref, e)

    row = lax.broadcasted_iota(jnp.int32, (SGU_CHUNK, SGU_CHUNK), 0)
    col = lax.broadcasted_iota(jnp.int32, (SGU_CHUNK, SGU_CHUNK), 1)
    causal = col <= row

    def out_chunk(j, _):
        z = _dot(h_ref[...], win_ref[2, j])
        mu = jnp.concatenate([mu_ref[...]] * per_chunk, axis=1)
        rstd = jnp.concatenate([rstd_ref[...]] * per_chunk, axis=1)
        v = ((c_ref[j] - mu) * rstd * lng_ref[j] + lnb_ref[j]).astype(BF16)
        w = jnp.where(causal, sw_ref[j], 0.0).astype(BF16)
        bias = jnp.concatenate([sb_ref[j]] * per_chunk, axis=1)
        col0 = pl.multiple_of(j * CN, CN)
        for rc in range(TM // SGU_CHUNK):
            rows = slice(rc * SGU_CHUNK, (rc + 1) * SGU_CHUNK)
            mixed = _dot(w, v[rows]) + bias
            y = u_ref[j, rows, :] * mixed * _silu(z[rows])
            y_ref[rows, pl.ds(col0, CN)] = y.astype(BF16)
        return 0

    lax.fori_loop(0, n_chunks, out_chunk, 0)

    _tail(x_ref, p_ref, y_ref, wout_ref, png_ref, gw_ref, pw_ref, fg_ref, o_ref, final)


def _resident(shape):
    zeros = (0,) * len(shape)
    return pl.BlockSpec(shape, lambda i: zeros, pipeline_mode=pl.Buffered(1))


def _layer_call(kernel_body, scratch, x2, p2, ng, win, wout, m0, m1, lng, lnb, png, gw, pw, fg):
    n, d = x2.shape
    ple = p2.shape[1]
    row_spec = lambda width: pl.BlockSpec((TM, width), lambda i: (i, 0))
    params = (ng, win, wout, m0, m1, lng, lnb, png, gw, pw, fg)
    return pl.pallas_call(
        kernel_body,
        grid=(n // TM,),
        in_specs=[row_spec(d), row_spec(ple)] + [_resident(a.shape) for a in params],
        out_specs=row_spec(d),
        out_shape=jax.ShapeDtypeStruct((n, d), F32),
        scratch_shapes=scratch,
        compiler_params=pltpu.CompilerParams(
            dimension_semantics=("arbitrary",),
            vmem_limit_bytes=VMEM_LIMIT_BYTES),
    )(x2, p2, *params)


def kernel(x, p, norm_g, w_in, w_out, conv_w, conv_b, conv_ln_g, conv_ln_b,
           sgu_ln_g, sgu_ln_b, sgu_w, sgu_b, pl_norm_g, pl_gate_w, pl_proj_w, final_g):
    bsz, seq, d = x.shape
    depth = w_in.shape[0]
    e = w_out.shape[1]
    ple = p.shape[-1]
    n = bsz * seq
    assert seq % TM == 0 and TM % SGU_CHUNK == 0 and e % CN == 0
    assert CN == e // sgu_w.shape[1] and sgu_w.shape[2] == SGU_CHUNK
    assert conv_w.shape[1] == CONV_K and HALO >= CONV_K - 1
    n_chunks = e // CN
    n_blocks = e // LANES

    stat = pltpu.VMEM((TM, LANES), F32)
    common = [stat, stat, stat, stat]
    conv_scratch = [pltpu.VMEM((TM, d), BF16),
                    pltpu.VMEM((n_blocks, TM + HALO, LANES), F32),
                    pltpu.VMEM((n_blocks, TM, LANES), F32),
                    pltpu.VMEM((TM, e), BF16)] + common
    sgu_scratch = [pltpu.VMEM((TM, d), BF16),
                   pltpu.VMEM((n_chunks, TM, CN), F32),
                   pltpu.VMEM((n_chunks, TM, CN), F32),
                   pltpu.VMEM((TM, e), BF16)] + common

    x2 = x.reshape(n, d)
    fg = final_g.reshape(1, d)
    for i in range(depth):
        j = i // 2
        final = i == depth - 1
        win = w_in[i].astype(BF16).reshape(d, 3, n_chunks, CN).transpose(1, 2, 0, 3)
        args = dict(
            x2=x2, p2=p[i].reshape(n, ple), ng=norm_g[i].reshape(1, d), win=win,
            wout=w_out[i].astype(BF16), png=pl_norm_g[i].reshape(1, d),
            gw=pl_gate_w[i].astype(BF16), pw=pl_proj_w[i].astype(BF16), fg=fg)
        if i % 2 == 0:
            body = functools.partial(_conv_layer_kernel, tiles_per_seq=seq // TM, final=final)
            x2 = _layer_call(
                body, conv_scratch,
                m0=conv_w[j].reshape(CONV_K, n_blocks, LANES).transpose(1, 0, 2),
                m1=conv_b[j].reshape(n_blocks, 1, LANES),
                lng=conv_ln_g[j].reshape(n_chunks, 1, CN),
                lnb=conv_ln_b[j].reshape(n_chunks, 1, CN), **args)
        else:
            body = functools.partial(_sgu_layer_kernel, final=final)
            x2 = _layer_call(
                body, sgu_scratch,
                m0=sgu_w[j],
                m1=jnp.broadcast_to(sgu_b[j][:, :, None], (e // CN, SGU_CHUNK, LANES)),
                lng=sgu_ln_g[j].reshape(n_chunks, 1, CN),
                lnb=sgu_ln_b[j].reshape(n_chunks, 1, CN), **args)
    return x2.reshape(bsz, seq, d)
```

```python
import functools

import jax
import jax.numpy as jnp
from jax import lax
from jax.experimental import pallas as pl
from jax.experimental.pallas import tpu as pltpu

F32 = jnp.float32
BF16 = jnp.bfloat16

EPS = 1e-6
CONV_K = 31
SGU_CHUNK = 128
LANES = 128
TM = 512
HALO = 32
CN = 256
PACK_ROWS = 32
CONV_ROWS = 16
CONV_RANGES = 2
G_PAD = 8
VMEM_LIMIT_BYTES = 56 * 1024 * 1024


def _dot(a, b):
    return jnp.dot(a, b, preferred_element_type=F32)


def _rms(x, g):
    ms = jnp.mean(x * x, axis=-1, keepdims=True)
    return x * lax.rsqrt(ms + EPS) * g


def _silu(x):
    return x * jax.nn.sigmoid(x)


def _gelu(x):
    return 0.5 * x * (1.0 + lax.erf(x * (2.0 ** -0.5)))


def _norm_rows(x_ref, g_ref, h_ref):
    rows = 64

    def body(i, _):
        r = pl.multiple_of(i * rows, rows)
        xb = x_ref[pl.ds(r, rows), :]
        h_ref[pl.ds(r, rows), :] = _rms(xb, g_ref[...]).astype(BF16)
        return 0

    lax.fori_loop(0, TM // rows, body, 0)


def _finish_stats(s1_ref, s2_ref, mu_ref, rstd_ref, width):
    mu = jnp.sum(s1_ref[...], axis=-1, keepdims=True) * (1.0 / width)
    ex2 = jnp.sum(s2_ref[...], axis=-1, keepdims=True) * (1.0 / width)
    var = ex2 - mu * mu
    mu_ref[...] = jnp.broadcast_to(mu, (TM, LANES))
    rstd_ref[...] = jnp.broadcast_to(lax.rsqrt(var + EPS), (TM, LANES))


def _tail(x_ref, p_ref, y_ref, wout_ref, png_ref, gw_ref, pw_ref, fg_ref, o_ref, final):
    half = y_ref.shape[1] // 2
    xo = x_ref[...] + _dot(y_ref[:, :half], wout_ref[:half, :]) + _dot(y_ref[:, half:], wout_ref[half:, :])
    r = _rms(xo, png_ref[...]).astype(BF16)
    gate = jax.nn.sigmoid(_dot(r, gw_ref[...]))
    pp = _dot(p_ref[...].astype(BF16), pw_ref[...])
    xo = xo + gate * pp
    if final:
        xo = _rms(xo, fg_ref[...])
    o_ref[...] = xo


def _conv_block(g_ref, pk_ref, cw_ref, cb_ref, c_ref, s1_ref, s2_ref, cb):
    for c in range((TM + HALO) // PACK_ROWS):
        r = c * PACK_ROWS
        w0 = c * PACK_ROWS // 2
        for parity in range(2):
            rows = g_ref[r + parity:r + parity + PACK_ROWS, :].astype(BF16)
            pk_ref[parity, w0:w0 + PACK_ROWS // 2, :] = pltpu.bitcast(rows, jnp.uint32)

    w = [cw_ref[cb, k] for k in range(CONV_K)]
    bias = cb_ref[cb]
    chains = TM // CONV_ROWS // CONV_RANGES
    accs = {}

    def mac(key, k, win):
        prod = w[k].astype(F32) * win
        accs[key] = accs[key] + prod if key in accs else prod

    for ph in range(chains + 1):
        for j in range(CONV_ROWS):
            for rg in range(CONV_RANGES):
                cur = ph < chains
                prev = ph >= 1 and j + CONV_ROWS < CONV_K
                if not (cur or prev):
                    continue
                s = (rg * chains + ph) * CONV_ROWS + HALO - (CONV_K - 1) + j
                win = pltpu.bitcast(pk_ref[s % 2, s // 2:s // 2 + CONV_ROWS // 2, :], BF16).astype(F32)
                if cur:
                    mac((rg, ph), j, win)
                if prev:
                    mac((rg, ph - 1), j + CONV_ROWS, win)
        if ph >= 1:
            for rg in range(CONV_RANGES):
                r0 = (rg * chains + ph - 1) * CONV_ROWS
                acc = accs.pop((rg, ph - 1)) + bias
                c_ref[cb, r0:r0 + CONV_ROWS, :] = acc
                s1_ref[r0:r0 + CONV_ROWS, :] += acc
                s2_ref[r0:r0 + CONV_ROWS, :] += acc * acc


def _conv_layer_kernel(x_ref, p_ref, ng_ref, win_ref, wout_ref, cw_ref, cb_ref,
                       lng_ref, lnb_ref, png_ref, gw_ref, pw_ref, fg_ref, o_ref,
                       h_ref, g_ref, pk_ref, c_ref, y_ref, s1_ref, s2_ref, mu_ref, rstd_ref,
                       *, tiles_per_seq, final):
    e = y_ref.shape[1]
    n_chunks = e // CN
    n_blocks = e // LANES
    per_chunk = CN // LANES

    first = pl.program_id(0) % tiles_per_seq == 0

    @pl.when(first)
    def _():
        g_ref[:, 0:HALO, :] = jnp.zeros((n_blocks, HALO, LANES), F32)

    @pl.when(jnp.logical_not(first))
    def _():
        g_ref[:, 0:HALO, :] = g_ref[:, TM:TM + HALO, :]

    g_ref[:, HALO + TM:HALO + TM + G_PAD, :] = jnp.zeros((n_blocks, G_PAD, LANES), F32)

    _norm_rows(x_ref, ng_ref, h_ref)

    s1_ref[...] = jnp.zeros((TM, LANES), F32)
    s2_ref[...] = jnp.zeros((TM, LANES), F32)

    def glu_chunk(j):
        h = h_ref[...]
        a = _dot(h, win_ref[0, j])
        b = _dot(h, win_ref[1, j])
        g = a * jax.nn.sigmoid(b)
        for l in range(per_chunk):
            g_ref[j * per_chunk + l, HALO:HALO + TM, :] = g[:, l * LANES:(l + 1) * LANES]

    def conv_chunk(j):
        for l in range(per_chunk):
            _conv_block(g_ref.at[j * per_chunk + l], pk_ref.at[l], cw_ref, cb_ref, c_ref, s1_ref, s2_ref,
                        j * per_chunk + l)

    glu_chunk(0)

    def glu_conv(j, _):
        conv_chunk(j - 1)
        glu_chunk(j)
        return 0

    lax.fori_loop(1, n_chunks, glu_conv, 0)

    def last_conv(l, _):
        cb = (n_chunks - 1) * per_chunk + l
        _conv_block(g_ref.at[cb], pk_ref.at[0], cw_ref, cb_ref, c_ref, s1_ref, s2_ref, cb)
        return 0

    lax.fori_loop(0, per_chunk, last_conv, 0)

    _finish_stats(s1_ref, s2_ref, mu_ref, rstd_ref, e)

    for j in range(n_chunks):
        z = _dot(h_ref[...], win_ref[2, j])
        lng = lng_ref[j]
        lnb = lnb_ref[j]
        for l in range(per_chunk):
            sl = slice(l * LANES, (l + 1) * LANES)
            yn = (c_ref[j * per_chunk + l] - mu_ref[...]) * rstd_ref[...] * lng[:, sl] + lnb[:, sl]
            y = _silu(yn) * _silu(z[:, sl])
            col = j * CN + l * LANES
            y_ref[:, col:col + LANES] = y.astype(BF16)

    _tail(x_ref, p_ref, y_ref, wout_ref, png_ref, gw_ref, pw_ref, fg_ref, o_ref, final)


def _sgu_layer_kernel(x_ref, p_ref, ng_ref, win_ref, wout_ref, sw_ref, sb_ref,
                      lng_ref, lnb_ref, png_ref, gw_ref, pw_ref, fg_ref, o_ref,
                      h_ref, u_ref, c_ref, y_ref, s1_ref, s2_ref, mu_ref, rstd_ref,
                      *, final):
    e = y_ref.shape[1]
    n_chunks = e // CN
    per_chunk = CN // LANES

    _norm_rows(x_ref, ng_ref, h_ref)

    s1_ref[...] = jnp.zeros((TM, LANES), F32)
    s2_ref[...] = jnp.zeros((TM, LANES), F32)

    def gelu_chunk(j, _):
        h = h_ref[...]
        u_ref[j] = _gelu(_dot(h, win_ref[0, j]))
        gb = _gelu(_dot(h, win_ref[1, j]))
        c_ref[j] = gb
        s1 = s1_ref[...]
        s2 = s2_ref[...]
        for l in range(per_chunk):
            blk = gb[:, l * LANES:(l + 1) * LANES]
            s1 = s1 + blk
            s2 = s2 + blk * blk
        s1_ref[...] = s1
        s2_ref[...] = s2
        return 0

    lax.fori_loop(0, n_chunks, gelu_chunk, 0, unroll=2)

    _finish_stats(s1_ref, s2_ref, mu_ref, rstd_ref, e)

    row = lax.broadcasted_iota(jnp.int32, (SGU_CHUNK, SGU_CHUNK), 0)
    col = lax.broadcasted_iota(jnp.int32, (SGU_CHUNK, SGU_CHUNK), 1)
    causal = col <= row

    for j in range(n_chunks):
        z = _dot(h_ref[...], win_ref[2, j])
        mu = jnp.concatenate([mu_ref[...]] * per_chunk, axis=1)
        rstd = jnp.concatenate([rstd_ref[...]] * per_chunk, axis=1)
        v = ((c_ref[j] - mu) * rstd * lng_ref[j] + lnb_ref[j]).astype(BF16)
        w = jnp.where(causal, sw_ref[j], 0.0).astype(BF16)
        bias = jnp.concatenate([sb_ref[j]] * per_chunk, axis=1)
        for rc in range(TM // SGU_CHUNK):
            rows = slice(rc * SGU_CHUNK, (rc + 1) * SGU_CHUNK)
            mixed = _dot(w, v[rows]) + bias
            y = u_ref[j, rows, :] * mixed * _silu(z[rows])
            y_ref[rows, j * CN:(j + 1) * CN] = y.astype(BF16)

    _tail(x_ref, p_ref, y_ref, wout_ref, png_ref, gw_ref, pw_ref, fg_ref, o_ref, final)


def _resident(shape):
    zeros = (0,) * len(shape)
    return pl.BlockSpec(shape, lambda i: zeros, pipeline_mode=pl.Buffered(1))


def _layer_call(kernel_body, name, scratch, x2, p2, ng, win, wout, m0, m1, lng, lnb, png, gw, pw, fg):
    n, d = x2.shape
    ple = p2.shape[1]
    row_spec = lambda width: pl.BlockSpec((TM, width), lambda i: (i, 0))
    params = (ng, win, wout, m0, m1, lng, lnb, png, gw, pw, fg)
    return pl.pallas_call(
        kernel_body,
        name=name,
        grid=(n // TM,),
        in_specs=[row_spec(d), row_spec(ple)] + [_resident(a.shape) for a in params],
        out_specs=row_spec(d),
        out_shape=jax.ShapeDtypeStruct((n, d), F32),
        scratch_shapes=scratch,
        compiler_params=pltpu.CompilerParams(
            dimension_semantics=("arbitrary",),
            vmem_limit_bytes=VMEM_LIMIT_BYTES),
    )(x2, p2, *params)


def kernel(x, p, norm_g, w_in, w_out, conv_w, conv_b, conv_ln_g, conv_ln_b,
           sgu_ln_g, sgu_ln_b, sgu_w, sgu_b, pl_norm_g, pl_gate_w, pl_proj_w, final_g):
    bsz, seq, d = x.shape
    depth = w_in.shape[0]
    e = w_out.shape[1]
    ple = p.shape[-1]
    n = bsz * seq
    assert seq % TM == 0 and TM % SGU_CHUNK == 0 and e % CN == 0
    assert CN == e // sgu_w.shape[1] and sgu_w.shape[2] == SGU_CHUNK
    assert conv_w.shape[1] == CONV_K and HALO >= CONV_K - 1
    n_chunks = e // CN
    n_blocks = e // LANES

    stat = pltpu.VMEM((TM, LANES), F32)
    common = [stat, stat, stat, stat]
    conv_scratch = [pltpu.VMEM((TM, d), BF16),
                    pltpu.VMEM((n_blocks, HALO + TM + G_PAD, LANES), F32),
                    pltpu.VMEM((CN // LANES, 2, (HALO + TM) // 2, LANES), jnp.uint32),
                    pltpu.VMEM((n_blocks, TM, LANES), F32),
                    pltpu.VMEM((TM, e), BF16)] + common
    sgu_scratch = [pltpu.VMEM((TM, d), BF16),
                   pltpu.VMEM((n_chunks, TM, CN), F32),
                   pltpu.VMEM((n_chunks, TM, CN), F32),
                   pltpu.VMEM((TM, e), BF16)] + common

    x2 = x.reshape(n, d)
    fg = final_g.reshape(1, d)
    for i in range(depth):
        j = i // 2
        final = i == depth - 1
        win = w_in[i].astype(BF16).reshape(d, 3, n_chunks, CN).transpose(1, 2, 0, 3)
        args = dict(
            x2=x2, p2=p[i].reshape(n, ple), ng=norm_g[i].reshape(1, d), win=win,
            wout=w_out[i].astype(BF16), png=pl_norm_g[i].reshape(1, d),
            gw=pl_gate_w[i].astype(BF16), pw=pl_proj_w[i].astype(BF16), fg=fg)
        if i % 2 == 0:
            body = functools.partial(_conv_layer_kernel, tiles_per_seq=seq // TM, final=final)
            x2 = _layer_call(
                body, f"conv_layer{i}", conv_scratch,
                m0=jnp.broadcast_to(
                    conv_w[j].astype(BF16).reshape(CONV_K, n_blocks, 1, LANES).transpose(1, 0, 2, 3),
                    (n_blocks, CONV_K, CONV_ROWS, LANES)),
                m1=conv_b[j].reshape(n_blocks, 1, LANES),
                lng=conv_ln_g[j].reshape(n_chunks, 1, CN),
                lnb=conv_ln_b[j].reshape(n_chunks, 1, CN), **args)
        else:
            body = functools.partial(_sgu_layer_kernel, final=final)
            x2 = _layer_call(
                body, f"sgu_layer{i}", sgu_scratch,
                m0=sgu_w[j],
                m1=jnp.broadcast_to(sgu_b[j][:, :, None], (e // CN, SGU_CHUNK, LANES)),
                lng=sgu_ln_g[j].reshape(n_chunks, 1, CN),
                lnb=sgu_ln_b[j].reshape(n_chunks, 1, CN), **args)
    return x2.reshape(bsz, seq, d)
```

```python
import functools

import jax
import jax.numpy as jnp
from jax import lax
from jax.experimental import pallas as pl
from jax.experimental.pallas import tpu as pltpu

F32 = jnp.float32
BF16 = jnp.bfloat16

EPS = 1e-6
CONV_K = 31
SGU_CHUNK = 128
LANES = 128
TM = 512
HALO = 32
CN = 256
PACK_ROWS = 32
CONV_ROWS = 16
CONV_SPLIT = 16
G_PAD = 8
VMEM_LIMIT_BYTES = 56 * 1024 * 1024


def _dot(a, b):
    return jnp.dot(a, b, preferred_element_type=F32)


def _rms(x, g):
    ms = jnp.mean(x * x, axis=-1, keepdims=True)
    return x * lax.rsqrt(ms + EPS) * g


def _silu(x):
    return x * jax.nn.sigmoid(x)


def _gelu(x):
    return 0.5 * x * (1.0 + lax.erf(x * (2.0 ** -0.5)))


def _w_in_chunk(win_ref, part, j):
    return win_ref[part, j]


def _norm_rows(x_ref, g_ref, h_ref):
    rows = 128

    def body(i, _):
        r = pl.multiple_of(i * rows, rows)
        xb = x_ref[pl.ds(r, rows), :]
        h_ref[pl.ds(r, rows), :] = _rms(xb, g_ref[...]).astype(BF16)
        return 0

    lax.fori_loop(0, TM // rows, body, 0, unroll=2)


def _finish_stats(s1_ref, s2_ref, mu_ref, rstd_ref, width):
    mu = jnp.sum(s1_ref[...], axis=-1, keepdims=True) * (1.0 / width)
    ex2 = jnp.sum(s2_ref[...], axis=-1, keepdims=True) * (1.0 / width)
    var = ex2 - mu * mu
    mu_ref[...] = jnp.broadcast_to(mu, (TM, LANES))
    rstd_ref[...] = jnp.broadcast_to(lax.rsqrt(var + EPS), (TM, LANES))


def _tail(x_ref, p_ref, y_ref, wout_ref, png_ref, gw_ref, pw_ref, fg_ref, o_ref, final):
    half = y_ref.shape[1] // 2
    xo = x_ref[...] + _dot(y_ref[:, :half], wout_ref[:half, :]) + _dot(y_ref[:, half:], wout_ref[half:, :])
    r = _rms(xo, png_ref[...]).astype(BF16)
    gate = jax.nn.sigmoid(_dot(r, gw_ref[...]))
    pp = _dot(p_ref[...].astype(BF16), pw_ref[...])
    xo = xo + gate * pp
    if final:
        xo = _rms(xo, fg_ref[...])
    o_ref[...] = xo


def _conv_block(g_ref, pk_ref, cw_ref, cb_ref, c_ref, s1_ref, s2_ref, cb, split=CONV_SPLIT):
    for c in range((TM + HALO) // PACK_ROWS):
        r = c * PACK_ROWS
        w0 = c * PACK_ROWS // 2
        for parity in range(2):
            rows = g_ref[r + parity:r + parity + PACK_ROWS, :].astype(BF16)
            pk_ref[parity, w0:w0 + PACK_ROWS // 2, :] = pltpu.bitcast(rows, jnp.uint32)

    w = [cw_ref[cb, k] for k in range(CONV_K)]
    bias = cb_ref[cb]
    n_acc = TM // CONV_ROWS
    ranges = tuple(r for r in ((0, split), (split, n_acc)) if r[0] < r[1])
    accs = {}

    def mac(key, k, win):
        prod = w[k].astype(F32) * win
        accs[key] = accs[key] + prod if key in accs else prod

    for ph in range(max(hi - lo for lo, hi in ranges) + 1):
        for j in range(CONV_ROWS):
            for lo, hi in ranges:
                cur = lo + ph < hi
                prev = ph >= 1 and lo + ph <= hi and j + CONV_ROWS < CONV_K
                if not (cur or prev):
                    continue
                s = (lo + ph) * CONV_ROWS + HALO - (CONV_K - 1) + j
                win = pltpu.bitcast(pk_ref[s % 2, s // 2:s // 2 + CONV_ROWS // 2, :], BF16).astype(F32)
                if cur:
                    mac(lo + ph, j, win)
                if prev:
                    mac(lo + ph - 1, j + CONV_ROWS, win)
        for lo, hi in ranges:
            if ph >= 1 and lo + ph <= hi:
                r0 = (lo + ph - 1) * CONV_ROWS
                acc = accs.pop(lo + ph - 1) + bias
                c_ref[cb, r0:r0 + CONV_ROWS, :] = acc
                s1_ref[r0:r0 + CONV_ROWS, :] += acc
                s2_ref[r0:r0 + CONV_ROWS, :] += acc * acc
    assert not accs


def _conv_layer_kernel(x_ref, p_ref, ng_ref, win_ref, wout_ref, cw_ref, cb_ref,
                       lng_ref, lnb_ref, png_ref, gw_ref, pw_ref, fg_ref, o_ref,
                       h_ref, g_ref, pk_ref, c_ref, y_ref, s1_ref, s2_ref, mu_ref, rstd_ref,
                       *, tiles_per_seq, final):
    e = y_ref.shape[1]
    n_chunks = e // CN
    n_blocks = e // LANES
    per_chunk = CN // LANES

    first = pl.program_id(0) % tiles_per_seq == 0

    @pl.when(first)
    def _():
        g_ref[:, 0:HALO, :] = jnp.zeros((n_blocks, HALO, LANES), F32)

    @pl.when(jnp.logical_not(first))
    def _():
        g_ref[:, 0:HALO, :] = g_ref[:, TM:TM + HALO, :]

    g_ref[:, HALO + TM:HALO + TM + G_PAD, :] = jnp.zeros((n_blocks, G_PAD, LANES), F32)

    _norm_rows(x_ref, ng_ref, h_ref)

    s1_ref[...] = jnp.zeros((TM, LANES), F32)
    s2_ref[...] = jnp.zeros((TM, LANES), F32)

    def glu_chunk(j):
        h = h_ref[...]
        gate = jax.nn.sigmoid(_dot(h, _w_in_chunk(win_ref, 1, j)))
        g = _dot(h, _w_in_chunk(win_ref, 0, j)) * gate
        for l in range(per_chunk):
            g_ref[j * per_chunk + l, HALO:HALO + TM, :] = g[:, l * LANES:(l + 1) * LANES]

    def conv_chunk(j):
        for l in range(per_chunk):
            _conv_block(g_ref.at[j * per_chunk + l], pk_ref.at[l], cw_ref, cb_ref, c_ref, s1_ref, s2_ref,
                        j * per_chunk + l)

    glu_chunk(0)

    def glu_conv(j, _):
        conv_chunk(j - 1)
        glu_chunk(j)
        return 0

    lax.fori_loop(1, n_chunks, glu_conv, 0)

    def last_conv(l, _):
        cb = (n_chunks - 1) * per_chunk + l
        _conv_block(g_ref.at[cb], pk_ref.at[0], cw_ref, cb_ref, c_ref, s1_ref, s2_ref, cb, split=TM // CONV_ROWS)
        return 0

    lax.fori_loop(0, per_chunk, last_conv, 0)

    _finish_stats(s1_ref, s2_ref, mu_ref, rstd_ref, e)

    for j in range(n_chunks):
        z = _dot(h_ref[...], _w_in_chunk(win_ref, 2, j))
        lng = lng_ref[j]
        lnb = lnb_ref[j]
        for l in range(per_chunk):
            sl = slice(l * LANES, (l + 1) * LANES)
            yn = (c_ref[j * per_chunk + l] - mu_ref[...]) * rstd_ref[...] * lng[:, sl] + lnb[:, sl]
            y = _silu(yn) * _silu(z[:, sl])
            col = j * CN + l * LANES
            y_ref[:, col:col + LANES] = y.astype(BF16)

    _tail(x_ref, p_ref, y_ref, wout_ref, png_ref, gw_ref, pw_ref, fg_ref, o_ref, final)


def _sgu_layer_kernel(x_ref, p_ref, ng_ref, win_ref, wout_ref, sw_ref, sb_ref,
                      lng_ref, lnb_ref, png_ref, gw_ref, pw_ref, fg_ref, o_ref,
                      h_ref, u_ref, c_ref, y_ref, s1_ref, s2_ref, mu_ref, rstd_ref,
                      *, final):
    e = y_ref.shape[1]
    n_chunks = e // CN
    per_chunk = CN // LANES

    _norm_rows(x_ref, ng_ref, h_ref)

    s1_ref[...] = jnp.zeros((TM, LANES), F32)
    s2_ref[...] = jnp.zeros((TM, LANES), F32)

    def gelu_chunk(j, _):
        h = h_ref[...]
        u_ref[j] = _gelu(_dot(h, _w_in_chunk(win_ref, 0, j)))
        gb = _gelu(_dot(h, _w_in_chunk(win_ref, 1, j)))
        c_ref[j] = gb
        s1 = s1_ref[...]
        s2 = s2_ref[...]
        for l in range(per_chunk):
            blk = gb[:, l * LANES:(l + 1) * LANES]
            s1 = s1 + blk
            s2 = s2 + blk * blk
        s1_ref[...] = s1
        s2_ref[...] = s2
        return 0

    lax.fori_loop(0, n_chunks, gelu_chunk, 0, unroll=2)

    _finish_stats(s1_ref, s2_ref, mu_ref, rstd_ref, e)

    row = lax.broadcasted_iota(jnp.int32, (SGU_CHUNK, SGU_CHUNK), 0)
    col = lax.broadcasted_iota(jnp.int32, (SGU_CHUNK, SGU_CHUNK), 1)
    causal = col <= row

    for j in range(n_chunks):
        z = _dot(h_ref[...], _w_in_chunk(win_ref, 2, j))
        mu = jnp.concatenate([mu_ref[...]] * per_chunk, axis=1)
        rstd = jnp.concatenate([rstd_ref[...]] * per_chunk, axis=1)
        v = ((c_ref[j] - mu) * rstd * lng_ref[j] + lnb_ref[j]).astype(BF16)
        w = jnp.where(causal, sw_ref[j], 0.0).astype(BF16)
        bias = jnp.concatenate([sb_ref[j]] * per_chunk, axis=1)
        for rc in range(TM // SGU_CHUNK):
            rows = slice(rc * SGU_CHUNK, (rc + 1) * SGU_CHUNK)
            mixed = _dot(w, v[rows]) + bias
            y = u_ref[j, rows, :] * mixed * _silu(z[rows])
            y_ref[rows, j * CN:(j + 1) * CN] = y.astype(BF16)

    _tail(x_ref, p_ref, y_ref, wout_ref, png_ref, gw_ref, pw_ref, fg_ref, o_ref, final)


def _resident(shape):
    zeros = (0,) * len(shape)
    return pl.BlockSpec(shape, lambda i: zeros, pipeline_mode=pl.Buffered(1))


def _layer_call(kernel_body, name, scratch, layer, x2, p3, ng, win, wout, m0, m1, lng, lnb, png, gw, pw, fg):
    n, d = x2.shape
    ple = p3.shape[2]
    p_spec = pl.BlockSpec((None, TM, ple), lambda i: (layer, i, 0))
    row_spec = lambda width: pl.BlockSpec((TM, width), lambda i: (i, 0))
    params = (ng, win, wout, m0, m1, lng, lnb, png, gw, pw, fg)
    return pl.pallas_call(
        kernel_body,
        name=name,
        grid=(n // TM,),
        in_specs=[row_spec(d), p_spec] + [_resident(a.shape) for a in params],
        out_specs=row_spec(d),
        out_shape=jax.ShapeDtypeStruct((n, d), F32),
        scratch_shapes=scratch,
        compiler_params=pltpu.CompilerParams(
            dimension_semantics=("arbitrary",),
            vmem_limit_bytes=VMEM_LIMIT_BYTES),
    )(x2, p3, *params)


def kernel(x, p, norm_g, w_in, w_out, conv_w, conv_b, conv_ln_g, conv_ln_b,
           sgu_ln_g, sgu_ln_b, sgu_w, sgu_b, pl_norm_g, pl_gate_w, pl_proj_w, final_g):
    bsz, seq, d = x.shape
    depth = w_in.shape[0]
    e = w_out.shape[1]
    ple = p.shape[-1]
    n = bsz * seq
    assert seq % TM == 0 and TM % SGU_CHUNK == 0 and e % CN == 0
    assert CN == e // sgu_w.shape[1] and sgu_w.shape[2] == SGU_CHUNK
    assert conv_w.shape[1] == CONV_K and HALO >= CONV_K - 1
    n_chunks = e // CN
    n_blocks = e // LANES

    stat = pltpu.VMEM((TM, LANES), F32)
    common = [stat, stat, stat, stat]
    conv_scratch = [pltpu.VMEM((TM, d), BF16),
                    pltpu.VMEM((n_blocks, HALO + TM + G_PAD, LANES), F32),
                    pltpu.VMEM((CN // LANES, 2, (HALO + TM) // 2, LANES), jnp.uint32),
                    pltpu.VMEM((n_blocks, TM, LANES), F32),
                    pltpu.VMEM((TM, e), BF16)] + common
    sgu_scratch = [pltpu.VMEM((TM, d), BF16),
                   pltpu.VMEM((n_chunks, TM, CN), F32),
                   pltpu.VMEM((n_chunks, TM, CN), F32),
                   pltpu.VMEM((TM, e), BF16)] + common

    x2 = x.reshape(n, d)
    fg = final_g.reshape(1, d)
    for i in range(depth):
        j = i // 2
        final = i == depth - 1
        win = w_in[i].astype(BF16).reshape(d, 3, n_chunks, CN).transpose(1, 2, 0, 3)
        args = dict(
            layer=i, x2=x2, p3=p.reshape(depth, n, ple), ng=norm_g[i].reshape(1, d), win=win,
            wout=w_out[i].astype(BF16), png=pl_norm_g[i].reshape(1, d),
            gw=pl_gate_w[i].astype(BF16), pw=pl_proj_w[i].astype(BF16), fg=fg)
        if i % 2 == 0:
            body = functools.partial(_conv_layer_kernel, tiles_per_seq=seq // TM, final=final)
            x2 = _layer_call(
                body, f"conv_layer{i}", conv_scratch,
                m0=jnp.broadcast_to(
                    conv_w[j].astype(BF16).reshape(CONV_K, n_blocks, 1, LANES).transpose(1, 0, 2, 3),
                    (n_blocks, CONV_K, CONV_ROWS, LANES)),
                m1=conv_b[j].reshape(n_blocks, 1, LANES),
                lng=conv_ln_g[j].reshape(n_chunks, 1, CN),
                lnb=conv_ln_b[j].reshape(n_chunks, 1, CN), **args)
        else:
            body = functools.partial(_sgu_layer_kernel, final=final)
            x2 = _layer_call(
                body, f"sgu_layer{i}", sgu_scratch,
                m0=sgu_w[j],
                m1=jnp.broadcast_to(sgu_b[j][:, :, None], (e // CN, SGU_CHUNK, LANES)),
                lng=sgu_ln_g[j].reshape(n_chunks, 1, CN),
                lnb=sgu_ln_b[j].reshape(n_chunks, 1, CN), **args)
    return x2.reshape(bsz, seq, d)
```

```python
import functools

import jax
import jax.numpy as jnp
from jax import lax
from jax.experimental import pallas as pl
from jax.experimental.pallas import tpu as pltpu

F32 = jnp.float32
BF16 = jnp.bfloat16

EPS = 1e-6
CONV_K = 31
SGU_CHUNK = 128
LANES = 128
TM = 512
HALO = 32
CN = 256
PACK_ROWS = 32
CONV_ROWS = 16
CONV_SPLIT = 16
G_PAD = 8
VMEM_LIMIT_BYTES = 56 * 1024 * 1024


def _dot(a, b):
    return jnp.dot(a, b, preferred_element_type=F32)


def _rms(x, g):
    ms = jnp.mean(x * x, axis=-1, keepdims=True)
    return x * lax.rsqrt(ms + EPS) * g


def _sigmoid(x):
    return 0.5 * jnp.tanh(0.5 * x) + 0.5


def _silu(x):
    return x * _sigmoid(x)


def _gelu(x):
    return 0.5 * x * (1.0 + lax.erf(x * (2.0 ** -0.5)))


def _w_in_chunk(win_ref, part, j):
    return win_ref[part, j]


def _norm_rows(x_ref, g_ref, h_ref):
    rows = 128

    def body(i, _):
        r = pl.multiple_of(i * rows, rows)
        xb = x_ref[pl.ds(r, rows), :]
        h_ref[pl.ds(r, rows), :] = _rms(xb, g_ref[...]).astype(BF16)
        return 0

    lax.fori_loop(0, TM // rows, body, 0, unroll=2)


def _finish_stats(s1_ref, s2_ref, mu_ref, rstd_ref, width):
    mu = jnp.sum(s1_ref[...], axis=-1, keepdims=True) * (1.0 / width)
    ex2 = jnp.sum(s2_ref[...], axis=-1, keepdims=True) * (1.0 / width)
    var = ex2 - mu * mu
    mu_ref[...] = jnp.broadcast_to(mu, (TM, LANES))
    rstd_ref[...] = jnp.broadcast_to(lax.rsqrt(var + EPS), (TM, LANES))


def _tail(x_ref, p_ref, y_ref, wout_ref, png_ref, gw_ref, pw_ref, fg_ref, o_ref, final):
    half = y_ref.shape[1] // 2
    xo = x_ref[...] + _dot(y_ref[:, :half], wout_ref[:half, :]) + _dot(y_ref[:, half:], wout_ref[half:, :])
    r = _rms(xo, png_ref[...]).astype(BF16)
    gate = _sigmoid(_dot(r, gw_ref[...]))
    pp = _dot(p_ref[...].astype(BF16), pw_ref[...])
    xo = xo + gate * pp
    if final:
        xo = _rms(xo, fg_ref[...])
    o_ref[...] = xo


def _conv_block(g_ref, pk_ref, cw_ref, cb_ref, c_ref, s1_ref, s2_ref, cb, split=CONV_SPLIT):
    for c in range((TM + HALO) // PACK_ROWS):
        r = c * PACK_ROWS
        w0 = c * PACK_ROWS // 2
        for parity in range(2):
            rows = g_ref[r + parity:r + parity + PACK_ROWS, :].astype(BF16)
            pk_ref[parity, w0:w0 + PACK_ROWS // 2, :] = pltpu.bitcast(rows, jnp.uint32)

    w = [cw_ref[cb, k] for k in range(CONV_K)]
    bias = cb_ref[cb]
    n_acc = TM // CONV_ROWS
    ranges = tuple(r for r in ((0, split), (split, n_acc)) if r[0] < r[1])
    accs = {}

    def mac(key, k, win):
        prod = w[k].astype(F32) * win
        accs[key] = accs[key] + prod if key in accs else prod

    for ph in range(max(hi - lo for lo, hi in ranges) + 1):
        for j in range(CONV_ROWS):
            for lo, hi in ranges:
                cur = lo + ph < hi
                prev = ph >= 1 and lo + ph <= hi and j + CONV_ROWS < CONV_K
                if not (cur or prev):
                    continue
                s = (lo + ph) * CONV_ROWS + HALO - (CONV_K - 1) + j
                win = pltpu.bitcast(pk_ref[s % 2, s // 2:s // 2 + CONV_ROWS // 2, :], BF16).astype(F32)
                if cur:
                    mac(lo + ph, j, win)
                if prev:
                    mac(lo + ph - 1, j + CONV_ROWS, win)
        for lo, hi in ranges:
            if ph >= 1 and lo + ph <= hi:
                r0 = (lo + ph - 1) * CONV_ROWS
                acc = accs.pop(lo + ph - 1) + bias
                c_ref[cb, r0:r0 + CONV_ROWS, :] = acc
                s1_ref[r0:r0 + CONV_ROWS, :] += acc
                s2_ref[r0:r0 + CONV_ROWS, :] += acc * acc
    assert not accs


def _conv_layer_kernel(x_ref, p_ref, ng_ref, win_ref, wout_ref, cw_ref, cb_ref,
                       lng_ref, lnb_ref, png_ref, gw_ref, pw_ref, fg_ref, o_ref,
                       h_ref, g_ref, pk_ref, c_ref, y_ref, s1_ref, s2_ref, mu_ref, rstd_ref,
                       *, tiles_per_seq, final):
    e = y_ref.shape[1]
    n_chunks = e // CN
    n_blocks = e // LANES
    per_chunk = CN // LANES

    first = pl.program_id(0) % tiles_per_seq == 0

    @pl.when(first)
    def _():
        g_ref[:, 0:HALO, :] = jnp.zeros((n_blocks, HALO, LANES), F32)

    @pl.when(jnp.logical_not(first))
    def _():
        g_ref[:, 0:HALO, :] = g_ref[:, TM:TM + HALO, :]

    g_ref[:, HALO + TM:HALO + TM + G_PAD, :] = jnp.zeros((n_blocks, G_PAD, LANES), F32)

    _norm_rows(x_ref, ng_ref, h_ref)

    s1_ref[...] = jnp.zeros((TM, LANES), F32)
    s2_ref[...] = jnp.zeros((TM, LANES), F32)

    def glu_chunk(j):
        h = h_ref[...]
        gate = _sigmoid(_dot(h, _w_in_chunk(win_ref, 1, j)))
        g = _dot(h, _w_in_chunk(win_ref, 0, j)) * gate
        for l in range(per_chunk):
            g_ref[j * per_chunk + l, HALO:HALO + TM, :] = g[:, l * LANES:(l + 1) * LANES]

    def conv_chunk(j):
        for l in range(per_chunk):
            _conv_block(g_ref.at[j * per_chunk + l], pk_ref.at[l], cw_ref, cb_ref, c_ref, s1_ref, s2_ref,
                        j * per_chunk + l)

    glu_chunk(0)

    def glu_conv(j, _):
        conv_chunk(j - 1)
        glu_chunk(j)
        return 0

    lax.fori_loop(1, n_chunks, glu_conv, 0)

    def last_conv(l, _):
        cb = (n_chunks - 1) * per_chunk + l
        _conv_block(g_ref.at[cb], pk_ref.at[0], cw_ref, cb_ref, c_ref, s1_ref, s2_ref, cb, split=TM // CONV_ROWS)
        return 0

    lax.fori_loop(0, per_chunk, last_conv, 0)

    _finish_stats(s1_ref, s2_ref, mu_ref, rstd_ref, e)

    for j in range(n_chunks):
        z = _dot(h_ref[...], _w_in_chunk(win_ref, 2, j))
        lng = lng_ref[j]
        lnb = lnb_ref[j]
        for l in range(per_chunk):
            sl = slice(l * LANES, (l + 1) * LANES)
            yn = (c_ref[j * per_chunk + l] - mu_ref[...]) * rstd_ref[...] * lng[:, sl] + lnb[:, sl]
            y = _silu(yn) * _silu(z[:, sl])
            col = j * CN + l * LANES
            y_ref[:, col:col + LANES] = y.astype(BF16)

    _tail(x_ref, p_ref, y_ref, wout_ref, png_ref, gw_ref, pw_ref, fg_ref, o_ref, final)


def _sgu_layer_kernel(x_ref, p_ref, ng_ref, win_ref, wout_ref, sw_ref, sb_ref,
                      lng_ref, lnb_ref, png_ref, gw_ref, pw_ref, fg_ref, o_ref,
                      h_ref, u_ref, c_ref, y_ref, s1_ref, s2_ref, mu_ref, rstd_ref,
                      *, final):
    e = y_ref.shape[1]
    n_chunks = e // CN
    per_chunk = CN // LANES

    _norm_rows(x_ref, ng_ref, h_ref)

    s1_ref[...] = jnp.zeros((TM, LANES), F32)
    s2_ref[...] = jnp.zeros((TM, LANES), F32)

    def gelu_chunk(j, _):
        h = h_ref[...]
        u_ref[j] = _gelu(_dot(h, _w_in_chunk(win_ref, 0, j)))
        gb = _gelu(_dot(h, _w_in_chunk(win_ref, 1, j)))
        c_ref[j] = gb
        s1 = s1_ref[...]
        s2 = s2_ref[...]
        for l in range(per_chunk):
            blk = gb[:, l * LANES:(l + 1) * LANES]
            s1 = s1 + blk
            s2 = s2 + blk * blk
        s1_ref[...] = s1
        s2_ref[...] = s2
        return 0

    lax.fori_loop(0, n_chunks, gelu_chunk, 0, unroll=4)

    _finish_stats(s1_ref, s2_ref, mu_ref, rstd_ref, e)

    row = lax.broadcasted_iota(jnp.int32, (SGU_CHUNK, SGU_CHUNK), 0)
    col = lax.broadcasted_iota(jnp.int32, (SGU_CHUNK, SGU_CHUNK), 1)
    causal = col <= row

    for j in range(n_chunks):
        z = _dot(h_ref[...], _w_in_chunk(win_ref, 2, j))
        mu = jnp.concatenate([mu_ref[...]] * per_chunk, axis=1)
        rstd = jnp.concatenate([rstd_ref[...]] * per_chunk, axis=1)
        v = ((c_ref[j] - mu) * rstd * lng_ref[j] + lnb_ref[j]).astype(BF16)
        w = jnp.where(causal, sw_ref[j], 0.0).astype(BF16)
        bias = jnp.concatenate([sb_ref[j]] * per_chunk, axis=1)
        for rc in range(TM // SGU_CHUNK):
            rows = slice(rc * SGU_CHUNK, (rc + 1) * SGU_CHUNK)
            mixed = _dot(w, v[rows]) + bias
            y = u_ref[j, rows, :] * mixed * _silu(z[rows])
            y_ref[rows, j * CN:(j + 1) * CN] = y.astype(BF16)

    _tail(x_ref, p_ref, y_ref, wout_ref, png_ref, gw_ref, pw_ref, fg_ref, o_ref, final)


def _resident(shape):
    zeros = (0,) * len(shape)
    return pl.BlockSpec(shape, lambda i: zeros, pipeline_mode=pl.Buffered(1))


def _layer_call(kernel_body, name, scratch, layer, x2, p3, ng, win, wout, m0, m1, lng, lnb, png, gw, pw, fg):
    n, d = x2.shape
    ple = p3.shape[2]
    p_spec = pl.BlockSpec((None, TM, ple), lambda i: (layer, i, 0))
    row_spec = lambda width: pl.BlockSpec((TM, width), lambda i: (i, 0))
    params = (ng, win, wout, m0, m1, lng, lnb, png, gw, pw, fg)
    return pl.pallas_call(
        kernel_body,
        name=name,
        grid=(n // TM,),
        in_specs=[row_spec(d), p_spec] + [_resident(a.shape) for a in params],
        out_specs=row_spec(d),
        out_shape=jax.ShapeDtypeStruct((n, d), F32),
        scratch_shapes=scratch,
        compiler_params=pltpu.CompilerParams(
            dimension_semantics=("arbitrary",),
            vmem_limit_bytes=VMEM_LIMIT_BYTES),
    )(x2, p3, *params)


def kernel(x, p, norm_g, w_in, w_out, conv_w, conv_b, conv_ln_g, conv_ln_b,
           sgu_ln_g, sgu_ln_b, sgu_w, sgu_b, pl_norm_g, pl_gate_w, pl_proj_w, final_g):
    bsz, seq, d = x.shape
    depth = w_in.shape[0]
    e = w_out.shape[1]
    ple = p.shape[-1]
    n = bsz * seq
    assert seq % TM == 0 and TM % SGU_CHUNK == 0 and e % CN == 0
    assert CN == e // sgu_w.shape[1] and sgu_w.shape[2] == SGU_CHUNK
    assert conv_w.shape[1] == CONV_K and HALO >= CONV_K - 1
    n_chunks = e // CN
    n_blocks = e // LANES

    stat = pltpu.VMEM((TM, LANES), F32)
    common = [stat, stat, stat, stat]
    conv_scratch = [pltpu.VMEM((TM, d), BF16),
                    pltpu.VMEM((n_blocks, HALO + TM + G_PAD, LANES), F32),
                    pltpu.VMEM((CN // LANES, 2, (HALO + TM) // 2, LANES), jnp.uint32),
                    pltpu.VMEM((n_blocks, TM, LANES), F32),
                    pltpu.VMEM((TM, e), BF16)] + common
    sgu_scratch = [pltpu.VMEM((TM, d), BF16),
                   pltpu.VMEM((n_chunks, TM, CN), F32),
                   pltpu.VMEM((n_chunks, TM, CN), F32),
                   pltpu.VMEM((TM, e), BF16)] + common

    x2 = x.reshape(n, d)
    fg = final_g.reshape(1, d)
    for i in range(depth):
        j = i // 2
        final = i == depth - 1
        win = w_in[i].astype(BF16).reshape(d, 3, n_chunks, CN).transpose(1, 2, 0, 3)
        args = dict(
            layer=i, x2=x2, p3=p.reshape(depth, n, ple), ng=norm_g[i].reshape(1, d), win=win,
            wout=w_out[i].astype(BF16), png=pl_norm_g[i].reshape(1, d),
            gw=pl_gate_w[i].astype(BF16), pw=pl_proj_w[i].astype(BF16), fg=fg)
        if i % 2 == 0:
            body = functools.partial(_conv_layer_kernel, tiles_per_seq=seq // TM, final=final)
            x2 = _layer_call(
                body, f"conv_layer{i}", conv_scratch,
                m0=jnp.broadcast_to(
                    conv_w[j].astype(BF16).reshape(CONV_K, n_blocks, 1, LANES).transpose(1, 0, 2, 3),
                    (n_blocks, CONV_K, CONV_ROWS, LANES)),
                m1=conv_b[j].reshape(n_blocks, 1, LANES),
                lng=conv_ln_g[j].reshape(n_chunks, 1, CN),
                lnb=conv_ln_b[j].reshape(n_chunks, 1, CN), **args)
        else:
            body = functools.partial(_sgu_layer_kernel, final=final)
            x2 = _layer_call(
                body, f"sgu_layer{i}", sgu_scratch,
                m0=sgu_w[j],
                m1=jnp.broadcast_to(sgu_b[j][:, :, None], (e // CN, SGU_CHUNK, LANES)),
                lng=sgu_ln_g[j].reshape(n_chunks, 1, CN),
                lnb=sgu_ln_b[j].reshape(n_chunks, 1, CN), **args)
    return x2.reshape(bsz, seq, d)
```

```python
import functools

import jax
import jax.numpy as jnp
from jax import lax
from jax.experimental import pallas as pl
from jax.experimental.pallas import tpu as pltpu

F32 = jnp.float32
BF16 = jnp.bfloat16

EPS = 1e-6
CONV_K = 31
SGU_CHUNK = 128
LANES = 128
TM = 512
HALO = 32
CN = 256
PACK_ROWS = 32
CONV_ROWS = 16
CONV_SPLIT = 16
G_PAD = 8
VMEM_LIMIT_BYTES = 56 * 1024 * 1024


def _dot(a, b):
    return jnp.dot(a, b, preferred_element_type=F32)


def _rms(x, g):
    ms = jnp.mean(x * x, axis=-1, keepdims=True)
    return x * lax.rsqrt(ms + EPS) * g


def _sigmoid(x):
    return 0.5 * jnp.tanh(0.5 * x) + 0.5


def _silu(x):
    return x * _sigmoid(x)


def _gelu(x):
    return 0.5 * x * (1.0 + lax.erf(x * (2.0 ** -0.5)))


def _w_in_chunk(win_ref, part, j):
    return win_ref[part, j]


def _norm_rows(x_ref, g_ref, h_ref):
    rows = 128

    def body(i, _):
        r = pl.multiple_of(i * rows, rows)
        xb = x_ref[pl.ds(r, rows), :]
        h_ref[pl.ds(r, rows), :] = _rms(xb, g_ref[...]).astype(BF16)
        return 0

    lax.fori_loop(0, TM // rows, body, 0, unroll=True)


def _finish_stats(s1_ref, s2_ref, mu_ref, rstd_ref, width):
    mu = jnp.sum(s1_ref[...], axis=-1, keepdims=True) * (1.0 / width)
    ex2 = jnp.sum(s2_ref[...], axis=-1, keepdims=True) * (1.0 / width)
    var = ex2 - mu * mu
    mu_ref[...] = jnp.broadcast_to(mu, (TM, LANES))
    rstd_ref[...] = jnp.broadcast_to(lax.rsqrt(var + EPS), (TM, LANES))


def _tail(x_ref, p_ref, y_ref, wout_ref, png_ref, gw_ref, pw_ref, fg_ref, o_ref, final):
    half = y_ref.shape[1] // 2
    xo = x_ref[...] + _dot(y_ref[:, :half], wout_ref[:half, :]) + _dot(y_ref[:, half:], wout_ref[half:, :])
    r = _rms(xo, png_ref[...]).astype(BF16)
    gate = _sigmoid(_dot(r, gw_ref[...]))
    pp = _dot(p_ref[...].astype(BF16), pw_ref[...])
    xo = xo + gate * pp
    if final:
        xo = _rms(xo, fg_ref[...])
    o_ref[...] = xo


def _conv_block(g_ref, pk_ref, cw_ref, cb_ref, c_ref, s1_ref, s2_ref, cb, split=CONV_SPLIT):
    for c in range((TM + HALO) // PACK_ROWS):
        r = c * PACK_ROWS
        w0 = c * PACK_ROWS // 2
        for parity in range(2):
            rows = g_ref[r + parity:r + parity + PACK_ROWS, :].astype(BF16)
            pk_ref[parity, w0:w0 + PACK_ROWS // 2, :] = pltpu.bitcast(rows, jnp.uint32)

    w = [cw_ref[cb, k] for k in range(CONV_K)]
    bias = cb_ref[cb]
    n_acc = TM // CONV_ROWS
    ranges = tuple(r for r in ((0, split), (split, n_acc)) if r[0] < r[1])
    accs = {}

    def mac(key, k, win):
        prod = w[k].astype(F32) * win
        accs[key] = accs[key] + prod if key in accs else prod

    for ph in range(max(hi - lo for lo, hi in ranges) + 1):
        for j in range(CONV_ROWS):
            for lo, hi in ranges:
                cur = lo + ph < hi
                prev = ph >= 1 and lo + ph <= hi and j + CONV_ROWS < CONV_K
                if not (cur or prev):
                    continue
                s = (lo + ph) * CONV_ROWS + HALO - (CONV_K - 1) + j
                win = pltpu.bitcast(pk_ref[s % 2, s // 2:s // 2 + CONV_ROWS // 2, :], BF16).astype(F32)
                if cur:
                    mac(lo + ph, j, win)
                if prev:
                    mac(lo + ph - 1, j + CONV_ROWS, win)
        for lo, hi in ranges:
            if ph >= 1 and lo + ph <= hi:
                r0 = (lo + ph - 1) * CONV_ROWS
                acc = accs.pop(lo + ph - 1) + bias
                c_ref[cb, r0:r0 + CONV_ROWS, :] = acc
                s1_ref[r0:r0 + CONV_ROWS, :] += acc
                s2_ref[r0:r0 + CONV_ROWS, :] += acc * acc
    assert not accs


def _conv_layer_kernel(x_ref, p_ref, ng_ref, win_ref, wout_ref, cw_ref, cb_ref,
                       lng_ref, lnb_ref, png_ref, gw_ref, pw_ref, fg_ref, o_ref,
                       h_ref, g_ref, pk_ref, c_ref, y_ref, s1_ref, s2_ref, mu_ref, rstd_ref,
                       *, tiles_per_seq, final):
    e = y_ref.shape[1]
    n_chunks = e // CN
    n_blocks = e // LANES
    per_chunk = CN // LANES

    first = pl.program_id(0) % tiles_per_seq == 0

    @pl.when(first)
    def _():
        g_ref[:, 0:HALO, :] = jnp.zeros((n_blocks, HALO, LANES), F32)

    @pl.when(jnp.logical_not(first))
    def _():
        g_ref[:, 0:HALO, :] = g_ref[:, TM:TM + HALO, :]

    g_ref[:, HALO + TM:HALO + TM + G_PAD, :] = jnp.zeros((n_blocks, G_PAD, LANES), F32)

    _norm_rows(x_ref, ng_ref, h_ref)

    s1_ref[...] = jnp.zeros((TM, LANES), F32)
    s2_ref[...] = jnp.zeros((TM, LANES), F32)

    def glu_chunk(j):
        h = h_ref[...]
        gate = _sigmoid(_dot(h, _w_in_chunk(win_ref, 1, j)))
        g = _dot(h, _w_in_chunk(win_ref, 0, j)) * gate
        for l in range(per_chunk):
            g_ref[j * per_chunk + l, HALO:HALO + TM, :] = g[:, l * LANES:(l + 1) * LANES]

    def conv_chunk(j):
        for l in range(per_chunk):
            _conv_block(g_ref.at[j * per_chunk + l], pk_ref.at[l], cw_ref, cb_ref, c_ref, s1_ref, s2_ref,
                        j * per_chunk + l)

    glu_chunk(0)

    def glu_conv(j, _):
        conv_chunk(j - 1)
        glu_chunk(j)
        return 0

    lax.fori_loop(1, n_chunks, glu_conv, 0)

    def last_conv(l, _):
        cb = (n_chunks - 1) * per_chunk + l
        _conv_block(g_ref.at[cb], pk_ref.at[0], cw_ref, cb_ref, c_ref, s1_ref, s2_ref, cb, split=TM // CONV_ROWS)
        return 0

    lax.fori_loop(0, per_chunk, last_conv, 0)

    _finish_stats(s1_ref, s2_ref, mu_ref, rstd_ref, e)

    for j in range(n_chunks):
        z = _dot(h_ref[...], _w_in_chunk(win_ref, 2, j))
        lng = lng_ref[j]
        lnb = lnb_ref[j]
        for l in range(per_chunk):
            sl = slice(l * LANES, (l + 1) * LANES)
            yn = (c_ref[j * per_chunk + l] - mu_ref[...]) * rstd_ref[...] * lng[:, sl] + lnb[:, sl]
            y = _silu(yn) * _silu(z[:, sl])
            col = j * CN + l * LANES
            y_ref[:, col:col + LANES] = y.astype(BF16)

    _tail(x_ref, p_ref, y_ref, wout_ref, png_ref, gw_ref, pw_ref, fg_ref, o_ref, final)


def _sgu_layer_kernel(x_ref, p_ref, ng_ref, win_ref, wout_ref, sw_ref, sb_ref,
                      lng_ref, lnb_ref, png_ref, gw_ref, pw_ref, fg_ref, o_ref,
                      h_ref, u_ref, c_ref, y_ref, s1_ref, s2_ref, mu_ref, rstd_ref,
                      *, final):
    e = y_ref.shape[1]
    n_chunks = e // CN
    per_chunk = CN // LANES

    _norm_rows(x_ref, ng_ref, h_ref)

    s1_ref[...] = jnp.zeros((TM, LANES), F32)
    s2_ref[...] = jnp.zeros((TM, LANES), F32)

    def gelu_chunk(j, _):
        h = h_ref[...]
        u_ref[j] = _gelu(_dot(h, _w_in_chunk(win_ref, 0, j)))
        gb = _gelu(_dot(h, _w_in_chunk(win_ref, 1, j)))
        c_ref[j] = gb
        s1 = s1_ref[...]
        s2 = s2_ref[...]
        for l in range(per_chunk):
            blk = gb[:, l * LANES:(l + 1) * LANES]
            s1 = s1 + blk
            s2 = s2 + blk * blk
        s1_ref[...] = s1
        s2_ref[...] = s2
        return 0

    lax.fori_loop(0, n_chunks, gelu_chunk, 0, unroll=8)

    _finish_stats(s1_ref, s2_ref, mu_ref, rstd_ref, e)

    row = lax.broadcasted_iota(jnp.int32, (SGU_CHUNK, SGU_CHUNK), 0)
    col = lax.broadcasted_iota(jnp.int32, (SGU_CHUNK, SGU_CHUNK), 1)
    causal = col <= row

    for j in range(n_chunks):
        z = _dot(h_ref[...], _w_in_chunk(win_ref, 2, j))
        mu = jnp.concatenate([mu_ref[...]] * per_chunk, axis=1)
        rstd = jnp.concatenate([rstd_ref[...]] * per_chunk, axis=1)
        v = ((c_ref[j] - mu) * rstd * lng_ref[j] + lnb_ref[j]).astype(BF16)
        w = jnp.where(causal, sw_ref[j], 0.0).astype(BF16)
        bias = jnp.concatenate([sb_ref[j]] * per_chunk, axis=1)
        for rc in range(TM // SGU_CHUNK):
            rows = slice(rc * SGU_CHUNK, (rc + 1) * SGU_CHUNK)
            mixed = _dot(w, v[rows]) + bias
            y = u_ref[j, rows, :] * mixed * _silu(z[rows])
            y_ref[rows, j * CN:(j + 1) * CN] = y.astype(BF16)

    _tail(x_ref, p_ref, y_ref, wout_ref, png_ref, gw_ref, pw_ref, fg_ref, o_ref, final)


def _resident(shape):
    zeros = (0,) * len(shape)
    return pl.BlockSpec(shape, lambda i: zeros, pipeline_mode=pl.Buffered(1))


def _layer_call(kernel_body, name, scratch, layer, x2, p3, ng, win, wout, m0, m1, lng, lnb, png, gw, pw, fg):
    n, d = x2.shape
    ple = p3.shape[2]
    p_spec = pl.BlockSpec((None, TM, ple), lambda i: (layer, i, 0))
    row_spec = lambda width: pl.BlockSpec((TM, width), lambda i: (i, 0))
    params = (ng, win, wout, m0, m1, lng, lnb, png, gw, pw, fg)
    return pl.pallas_call(
        kernel_body,
        name=name,
        grid=(n // TM,),
        in_specs=[row_spec(d), p_spec] + [_resident(a.shape) for a in params],
        out_specs=row_spec(d),
        out_shape=jax.ShapeDtypeStruct((n, d), F32),
        scratch_shapes=scratch,
        compiler_params=pltpu.CompilerParams(
            dimension_semantics=("arbitrary",),
            vmem_limit_bytes=VMEM_LIMIT_BYTES),
    )(x2, p3, *params)


def kernel(x, p, norm_g, w_in, w_out, conv_w, conv_b, conv_ln_g, conv_ln_b,
           sgu_ln_g, sgu_ln_b, sgu_w, sgu_b, pl_norm_g, pl_gate_w, pl_proj_w, final_g):
    bsz, seq, d = x.shape
    depth = w_in.shape[0]
    e = w_out.shape[1]
    ple = p.shape[-1]
    n = bsz * seq
    assert seq % TM == 0 and TM % SGU_CHUNK == 0 and e % CN == 0
    assert CN == e // sgu_w.shape[1] and sgu_w.shape[2] == SGU_CHUNK
    assert conv_w.shape[1] == CONV_K and HALO >= CONV_K - 1
    n_chunks = e // CN
    n_blocks = e // LANES

    stat = pltpu.VMEM((TM, LANES), F32)
    common = [stat, stat, stat, stat]
    conv_scratch = [pltpu.VMEM((TM, d), BF16),
                    pltpu.VMEM((n_blocks, HALO + TM + G_PAD, LANES), F32),
                    pltpu.VMEM((CN // LANES, 2, (HALO + TM) // 2, LANES), jnp.uint32),
                    pltpu.VMEM((n_blocks, TM, LANES), F32),
                    pltpu.VMEM((TM, e), BF16)] + common
    sgu_scratch = [pltpu.VMEM((TM, d), BF16),
                   pltpu.VMEM((n_chunks, TM, CN), F32),
                   pltpu.VMEM((n_chunks, TM, CN), F32),
                   pltpu.VMEM((TM, e), BF16)] + common

    x2 = x.reshape(n, d)
    fg = final_g.reshape(1, d)
    for i in range(depth):
        j = i // 2
        final = i == depth - 1
        win = w_in[i].astype(BF16).reshape(d, 3, n_chunks, CN).transpose(1, 2, 0, 3)
        args = dict(
            layer=i, x2=x2, p3=p.reshape(depth, n, ple), ng=norm_g[i].reshape(1, d), win=win,
            wout=w_out[i].astype(BF16), png=pl_norm_g[i].reshape(1, d),
            gw=pl_gate_w[i].astype(BF16), pw=pl_proj_w[i].astype(BF16), fg=fg)
        if i % 2 == 0:
            body = functools.partial(_conv_layer_kernel, tiles_per_seq=seq // TM, final=final)
            x2 = _layer_call(
                body, f"conv_layer{i}", conv_scratch,
                m0=jnp.broadcast_to(
                    conv_w[j].astype(BF16).reshape(CONV_K, n_blocks, 1, LANES).transpose(1, 0, 2, 3),
                    (n_blocks, CONV_K, CONV_ROWS, LANES)),
                m1=conv_b[j].reshape(n_blocks, 1, LANES),
                lng=conv_ln_g[j].reshape(n_chunks, 1, CN),
                lnb=conv_ln_b[j].reshape(n_chunks, 1, CN), **args)
        else:
            body = functools.partial(_sgu_layer_kernel, final=final)
            x2 = _layer_call(
                body, f"sgu_layer{i}", sgu_scratch,
                m0=sgu_w[j],
                m1=jnp.broadcast_to(sgu_b[j][:, :, None], (e // CN, SGU_CHUNK, LANES)),
                lng=sgu_ln_g[j].reshape(n_chunks, 1, CN),
                lnb=sgu_ln_b[j].reshape(n_chunks, 1, CN), **args)
    return x2.reshape(bsz, seq, d)
```

```python
import functools

import jax
import jax.numpy as jnp
from jax import lax
from jax.experimental import pallas as pl
from jax.experimental.pallas import tpu as pltpu

F32 = jnp.float32
BF16 = jnp.bfloat16

EPS = 1e-6
CONV_K = 31
SGU_CHUNK = 128
LANES = 128
TM = 512
HALO = 32
CN = 256
PACK_ROWS = 32
CONV_ROWS = 16
CONV_SPLIT = 16
G_PAD = 8
VMEM_LIMIT_BYTES = 56 * 1024 * 1024


def _dot(a, b):
    return jnp.dot(a, b, preferred_element_type=F32)


def _rms(x, g):
    ms = jnp.mean(x * x, axis=-1, keepdims=True)
    return x * lax.rsqrt(ms + EPS) * g


def _sigmoid(x):
    return 0.5 * jnp.tanh(0.5 * x) + 0.5


def _silu(x):
    return x * _sigmoid(x)


def _gelu(x):
    return 0.5 * x * (1.0 + lax.erf(x * (2.0 ** -0.5)))


def _w_in_chunk(win_ref, part, j):
    return win_ref[part, j]


def _norm_rows(x_ref, g_ref, h_ref):
    rows = 128

    def body(i, _):
        r = pl.multiple_of(i * rows, rows)
        xb = x_ref[pl.ds(r, rows), :]
        h_ref[pl.ds(r, rows), :] = _rms(xb, g_ref[...]).astype(BF16)
        return 0

    lax.fori_loop(0, TM // rows, body, 0, unroll=True)


def _finish_stats(s1_ref, s2_ref, mu_ref, rstd_ref, width):
    mu = jnp.sum(s1_ref[...], axis=-1, keepdims=True) * (1.0 / width)
    ex2 = jnp.sum(s2_ref[...], axis=-1, keepdims=True) * (1.0 / width)
    var = ex2 - mu * mu
    mu_ref[...] = jnp.broadcast_to(mu, (TM, LANES))
    rstd_ref[...] = jnp.broadcast_to(lax.rsqrt(var + EPS), (TM, LANES))


def _tail(x_ref, p_ref, y_ref, wout_ref, png_ref, gw_ref, pw_ref, fg_ref, o_ref, final):
    half = y_ref.shape[1] // 2
    xo = x_ref[...] + _dot(y_ref[:, :half], wout_ref[:half, :]) + _dot(y_ref[:, half:], wout_ref[half:, :])
    r = _rms(xo, png_ref[...]).astype(BF16)
    gate = _sigmoid(_dot(r, gw_ref[...]))
    pp = _dot(p_ref[...].astype(BF16), pw_ref[...])
    xo = xo + gate * pp
    if final:
        xo = _rms(xo, fg_ref[...])
    o_ref[...] = xo


def _conv_block(g_ref, pk_ref, cw_ref, cb_ref, c_ref, s1_ref, s2_ref, cb, split=CONV_SPLIT):
    for c in range((TM + HALO) // PACK_ROWS):
        r = c * PACK_ROWS
        w0 = c * PACK_ROWS // 2
        for parity in range(2):
            rows = g_ref[r + parity:r + parity + PACK_ROWS, :].astype(BF16)
            pk_ref[parity, w0:w0 + PACK_ROWS // 2, :] = pltpu.bitcast(rows, jnp.uint32)

    w = [cw_ref[cb, k] for k in range(CONV_K)]
    bias = cb_ref[cb]
    n_acc = TM // CONV_ROWS
    ranges = tuple(r for r in ((0, split), (split, n_acc)) if r[0] < r[1])
    accs = {}

    def mac(key, k, win):
        prod = w[k].astype(F32) * win
        accs[key] = accs[key] + prod if key in accs else prod

    for ph in range(max(hi - lo for lo, hi in ranges) + 1):
        for j in range(CONV_ROWS):
            for lo, hi in ranges:
                cur = lo + ph < hi
                prev = ph >= 1 and lo + ph <= hi and j + CONV_ROWS < CONV_K
                if not (cur or prev):
                    continue
                s = (lo + ph) * CONV_ROWS + HALO - (CONV_K - 1) + j
                win = pltpu.bitcast(pk_ref[s % 2, s // 2:s // 2 + CONV_ROWS // 2, :], BF16).astype(F32)
                if cur:
                    mac(lo + ph, j, win)
                if prev:
                    mac(lo + ph - 1, j + CONV_ROWS, win)
        for lo, hi in ranges:
            if ph >= 1 and lo + ph <= hi:
                r0 = (lo + ph - 1) * CONV_ROWS
                acc = accs.pop(lo + ph - 1) + bias
                c_ref[cb, r0:r0 + CONV_ROWS, :] = acc
                s1_ref[r0:r0 + CONV_ROWS, :] += acc
                s2_ref[r0:r0 + CONV_ROWS, :] += acc * acc
    assert not accs


def _conv_layer_kernel(x_ref, p_ref, ng_ref, win_ref, wout_ref, cw_ref, cb_ref,
                       lng_ref, lnb_ref, png_ref, gw_ref, pw_ref, fg_ref, o_ref,
                       h_ref, g_ref, pk_ref, c_ref, y_ref, s1_ref, s2_ref, mu_ref, rstd_ref,
                       *, tiles_per_seq, final):
    e = y_ref.shape[1]
    n_chunks = e // CN
    n_blocks = e // LANES
    per_chunk = CN // LANES

    first = pl.program_id(0) % tiles_per_seq == 0

    @pl.when(first)
    def _():
        g_ref[:, 0:HALO, :] = jnp.zeros((n_blocks, HALO, LANES), F32)

    @pl.when(jnp.logical_not(first))
    def _():
        g_ref[:, 0:HALO, :] = g_ref[:, TM:TM + HALO, :]

    g_ref[:, HALO + TM:HALO + TM + G_PAD, :] = jnp.zeros((n_blocks, G_PAD, LANES), F32)

    _norm_rows(x_ref, ng_ref, h_ref)

    s1_ref[...] = jnp.zeros((TM, LANES), F32)
    s2_ref[...] = jnp.zeros((TM, LANES), F32)

    def glu_chunk(j):
        h = h_ref[...]
        gate = _sigmoid(_dot(h, _w_in_chunk(win_ref, 1, j)))
        g = _dot(h, _w_in_chunk(win_ref, 0, j)) * gate
        for l in range(per_chunk):
            g_ref[j * per_chunk + l, HALO:HALO + TM, :] = g[:, l * LANES:(l + 1) * LANES]

    def conv_chunk(j):
        for l in range(per_chunk):
            _conv_block(g_ref.at[j * per_chunk + l], pk_ref.at[l], cw_ref, cb_ref, c_ref, s1_ref, s2_ref,
                        j * per_chunk + l)

    glu_chunk(0)

    def glu_conv(j, _):
        conv_chunk(j - 1)
        glu_chunk(j)
        return 0

    lax.fori_loop(1, n_chunks, glu_conv, 0)

    def last_conv(l, _):
        cb = (n_chunks - 1) * per_chunk + l
        _conv_block(g_ref.at[cb], pk_ref.at[0], cw_ref, cb_ref, c_ref, s1_ref, s2_ref, cb, split=TM // CONV_ROWS)
        return 0

    lax.fori_loop(0, per_chunk, last_conv, 0)

    _finish_stats(s1_ref, s2_ref, mu_ref, rstd_ref, e)

    for j in range(n_chunks):
        z = _dot(h_ref[...], _w_in_chunk(win_ref, 2, j))
        lng = lng_ref[j]
        lnb = lnb_ref[j]
        for l in range(per_chunk):
            sl = slice(l * LANES, (l + 1) * LANES)
            yn = (c_ref[j * per_chunk + l] - mu_ref[...]) * rstd_ref[...] * lng[:, sl] + lnb[:, sl]
            y = _silu(yn) * _silu(z[:, sl])
            col = j * CN + l * LANES
            y_ref[:, col:col + LANES] = y.astype(BF16)

    _tail(x_ref, p_ref, y_ref, wout_ref, png_ref, gw_ref, pw_ref, fg_ref, o_ref, final)


def _sgu_layer_kernel(x_ref, p_ref, ng_ref, win_ref, wout_ref, sw_ref, sb_ref,
                      lng_ref, lnb_ref, png_ref, gw_ref, pw_ref, fg_ref, o_ref,
                      h_ref, u_ref, c_ref, y_ref, s1_ref, s2_ref, mu_ref, rstd_ref,
                      *, final):
    e = y_ref.shape[1]
    n_chunks = e // CN
    per_chunk = CN // LANES

    _norm_rows(x_ref, ng_ref, h_ref)

    s1_ref[...] = jnp.zeros((TM, LANES), F32)
    s2_ref[...] = jnp.zeros((TM, LANES), F32)

    def gelu_chunk(j, _):
        h = h_ref[...]
        u_ref[j] = _gelu(_dot(h, _w_in_chunk(win_ref, 0, j)))
        gb = _gelu(_dot(h, _w_in_chunk(win_ref, 1, j)))
        c_ref[j] = gb
        s1 = s1_ref[...]
        s2 = s2_ref[...]
        for l in range(per_chunk):
            blk = gb[:, l * LANES:(l + 1) * LANES]
            s1 = s1 + blk
            s2 = s2 + blk * blk
        s1_ref[...] = s1
        s2_ref[...] = s2
        return 0

    lax.fori_loop(0, n_chunks, gelu_chunk, 0, unroll=8)

    _finish_stats(s1_ref, s2_ref, mu_ref, rstd_ref, e)

    row = lax.broadcasted_iota(jnp.int32, (SGU_CHUNK, SGU_CHUNK), 0)
    col = lax.broadcasted_iota(jnp.int32, (SGU_CHUNK, SGU_CHUNK), 1)
    causal = col <= row

    for j in range(n_chunks):
        z = _dot(h_ref[...], _w_in_chunk(win_ref, 2, j))
        mu = jnp.concatenate([mu_ref[...]] * per_chunk, axis=1)
        rstd = jnp.concatenate([rstd_ref[...]] * per_chunk, axis=1)
        v = ((c_ref[j] - mu) * rstd * lng_ref[j] + lnb_ref[j]).astype(BF16)
        w = jnp.where(causal, sw_ref[j], 0.0).astype(BF16)
        bias = jnp.concatenate([sb_ref[j]] * per_chunk, axis=1)
        for rc in range(TM // SGU_CHUNK):
            rows = slice(rc * SGU_CHUNK, (rc + 1) * SGU_CHUNK)
            mixed = _dot(w, v[rows]) + bias
            y = u_ref[j, rows, :] * mixed * _silu(z[rows])
            y_ref[rows, j * CN:(j + 1) * CN] = y.astype(BF16)

    _tail(x_ref, p_ref, y_ref, wout_ref, png_ref, gw_ref, pw_ref, fg_ref, o_ref, final)


def _resident(shape):
    zeros = (0,) * len(shape)
    return pl.BlockSpec(shape, lambda i: zeros, pipeline_mode=pl.Buffered(1))


def _resident_layer(shape, layer):
    zeros = (0,) * (len(shape) - 1)
    return pl.BlockSpec((None,) + tuple(shape[1:]), lambda i: (layer,) + zeros, pipeline_mode=pl.Buffered(1))


def _layer_call(kernel_body, name, scratch, layer, x2, p3, ng, win, wout, m0, m1, lng, lnb, png, gw, pw, fg):
    n, d = x2.shape
    ple = p3.shape[2]
    p_spec = pl.BlockSpec((None, TM, ple), lambda i: (layer, i, 0))
    row_spec = lambda width: pl.BlockSpec((TM, width), lambda i: (i, 0))
    params = (ng, win, wout, m0, m1, lng, lnb, png, gw, pw, fg)
    stacked = (1, 2, 8, 9)
    param_specs = [_resident_layer(a.shape, layer) if k in stacked else _resident(a.shape)
                   for k, a in enumerate(params)]
    return pl.pallas_call(
        kernel_body,
        name=name,
        grid=(n // TM,),
        in_specs=[row_spec(d), p_spec] + param_specs,
        out_specs=row_spec(d),
        out_shape=jax.ShapeDtypeStruct((n, d), F32),
        scratch_shapes=scratch,
        compiler_params=pltpu.CompilerParams(
            dimension_semantics=("arbitrary",),
            vmem_limit_bytes=VMEM_LIMIT_BYTES),
    )(x2, p3, *params)


def kernel(x, p, norm_g, w_in, w_out, conv_w, conv_b, conv_ln_g, conv_ln_b,
           sgu_ln_g, sgu_ln_b, sgu_w, sgu_b, pl_norm_g, pl_gate_w, pl_proj_w, final_g):
    bsz, seq, d = x.shape
    depth = w_in.shape[0]
    e = w_out.shape[1]
    ple = p.shape[-1]
    n = bsz * seq
    assert seq % TM == 0 and TM % SGU_CHUNK == 0 and e % CN == 0
    assert CN == e // sgu_w.shape[1] and sgu_w.shape[2] == SGU_CHUNK
    assert conv_w.shape[1] == CONV_K and HALO >= CONV_K - 1
    n_chunks = e // CN
    n_blocks = e // LANES

    stat = pltpu.VMEM((TM, LANES), F32)
    common = [stat, stat, stat, stat]
    conv_scratch = [pltpu.VMEM((TM, d), BF16),
                    pltpu.VMEM((n_blocks, HALO + TM + G_PAD, LANES), F32),
                    pltpu.VMEM((CN // LANES, 2, (HALO + TM) // 2, LANES), jnp.uint32),
                    pltpu.VMEM((n_blocks, TM, LANES), F32),
                    pltpu.VMEM((TM, e), BF16)] + common
    sgu_scratch = [pltpu.VMEM((TM, d), BF16),
                   pltpu.VMEM((n_chunks, TM, CN), F32),
                   pltpu.VMEM((n_chunks, TM, CN), F32),
                   pltpu.VMEM((TM, e), BF16)] + common

    x2 = x.reshape(n, d)
    fg = final_g.reshape(1, d)
    win = w_in.astype(BF16).reshape(depth, d, 3, n_chunks, CN).transpose(0, 2, 3, 1, 4)
    wout = w_out.astype(BF16)
    gw = pl_gate_w.astype(BF16)
    pw = pl_proj_w.astype(BF16)
    for i in range(depth):
        j = i // 2
        final = i == depth - 1
        args = dict(
            layer=i, x2=x2, p3=p.reshape(depth, n, ple), ng=norm_g[i].reshape(1, d), win=win,
            wout=wout, png=pl_norm_g[i].reshape(1, d), gw=gw, pw=pw, fg=fg)
        if i % 2 == 0:
            body = functools.partial(_conv_layer_kernel, tiles_per_seq=seq // TM, final=final)
            x2 = _layer_call(
                body, f"conv_layer{i}", conv_scratch,
                m0=jnp.broadcast_to(
                    conv_w[j].astype(BF16).reshape(CONV_K, n_blocks, 1, LANES).transpose(1, 0, 2, 3),
                    (n_blocks, CONV_K, CONV_ROWS, LANES)),
                m1=conv_b[j].reshape(n_blocks, 1, LANES),
                lng=conv_ln_g[j].reshape(n_chunks, 1, CN),
                lnb=conv_ln_b[j].reshape(n_chunks, 1, CN), **args)
        else:
            body = functools.partial(_sgu_layer_kernel, final=final)
            x2 = _layer_call(
                body, f"sgu_layer{i}", sgu_scratch,
                m0=sgu_w[j],
                m1=jnp.broadcast_to(sgu_b[j][:, :, None], (e // CN, SGU_CHUNK, LANES)),
                lng=sgu_ln_g[j].reshape(n_chunks, 1, CN),
                lnb=sgu_ln_b[j].reshape(n_chunks, 1, CN), **args)
    return x2.reshape(bsz, seq, d)
```

```python
import functools

import jax
import jax.numpy as jnp
from jax import lax
from jax.experimental import pallas as pl
from jax.experimental.pallas import tpu as pltpu

F32 = jnp.float32
BF16 = jnp.bfloat16

EPS = 1e-6
N_MIXERS = 2
CONV_K = 31
SGU_CHUNK = 128
LANES = 128
TM = 512
HALO = 32
CN = 256
PACK_ROWS = 32
CONV_ROWS = 16
CONV_SPLIT = 16
G_PAD = 8
VMEM_LIMIT_BYTES = 56 * 1024 * 1024


def _dot(a, b):
    return jnp.dot(a, b, preferred_element_type=F32)


def _rms(x, g):
    ms = jnp.mean(x * x, axis=-1, keepdims=True)
    return x * lax.rsqrt(ms + EPS) * g


def _sigmoid(x):
    return 0.5 * jnp.tanh(0.5 * x) + 0.5


def _silu(x):
    return x * _sigmoid(x)


def _gelu(x):
    return 0.5 * x * (1.0 + lax.erf(x * (2.0 ** -0.5)))


def _w_in_chunk(win_ref, part, j):
    return win_ref[part, j]


def _norm_rows(x_ref, g_ref, h_ref):
    rows = 128

    def body(i, _):
        r = pl.multiple_of(i * rows, rows)
        xb = x_ref[pl.ds(r, rows), :]
        h_ref[pl.ds(r, rows), :] = _rms(xb, g_ref[...]).astype(BF16)
        return 0

    lax.fori_loop(0, TM // rows, body, 0, unroll=True)


def _finish_stats(s1_ref, s2_ref, mu_ref, rstd_ref, width):
    mu = jnp.sum(s1_ref[...], axis=-1, keepdims=True) * (1.0 / width)
    ex2 = jnp.sum(s2_ref[...], axis=-1, keepdims=True) * (1.0 / width)
    var = ex2 - mu * mu
    mu_ref[...] = jnp.broadcast_to(mu, (TM, LANES))
    rstd_ref[...] = jnp.broadcast_to(lax.rsqrt(var + EPS), (TM, LANES))


def _tail(x_ref, p_ref, y_ref, wout_ref, png_ref, gw_ref, pw_ref, fg_ref, o_ref, final):
    half = y_ref.shape[1] // 2
    xo = x_ref[...] + _dot(y_ref[:, :half], wout_ref[:half, :]) + _dot(y_ref[:, half:], wout_ref[half:, :])
    r = _rms(xo, png_ref[...]).astype(BF16)
    gate = _sigmoid(_dot(r, gw_ref[...]))
    pp = _dot(p_ref[...].astype(BF16), pw_ref[...])
    xo = xo + gate * pp
    if final:
        xo = _rms(xo, fg_ref[...])
    o_ref[...] = xo


def _conv_block(g_ref, pk_ref, cw_ref, cb_ref, c_ref, s1_ref, s2_ref, cb, split=CONV_SPLIT):
    for c in range((TM + HALO) // PACK_ROWS):
        r = c * PACK_ROWS
        w0 = c * PACK_ROWS // 2
        for parity in range(2):
            rows = g_ref[r + parity:r + parity + PACK_ROWS, :].astype(BF16)
            pk_ref[parity, w0:w0 + PACK_ROWS // 2, :] = pltpu.bitcast(rows, jnp.uint32)

    w = [cw_ref[cb, k] for k in range(CONV_K)]
    bias = cb_ref[cb]
    n_acc = TM // CONV_ROWS
    ranges = tuple(r for r in ((0, split), (split, n_acc)) if r[0] < r[1])
    accs = {}

    def mac(key, k, win):
        prod = w[k].astype(F32) * win
        accs[key] = accs[key] + prod if key in accs else prod

    for ph in range(max(hi - lo for lo, hi in ranges) + 1):
        for j in range(CONV_ROWS):
            for lo, hi in ranges:
                cur = lo + ph < hi
                prev = ph >= 1 and lo + ph <= hi and j + CONV_ROWS < CONV_K
                if not (cur or prev):
                    continue
                s = (lo + ph) * CONV_ROWS + HALO - (CONV_K - 1) + j
                win = pltpu.bitcast(pk_ref[s % 2, s // 2:s // 2 + CONV_ROWS // 2, :], BF16).astype(F32)
                if cur:
                    mac(lo + ph, j, win)
                if prev:
                    mac(lo + ph - 1, j + CONV_ROWS, win)
        for lo, hi in ranges:
            if ph >= 1 and lo + ph <= hi:
                r0 = (lo + ph - 1) * CONV_ROWS
                acc = accs.pop(lo + ph - 1) + bias
                c_ref[cb, r0:r0 + CONV_ROWS, :] = acc
                s1_ref[r0:r0 + CONV_ROWS, :] += acc
                s2_ref[r0:r0 + CONV_ROWS, :] += acc * acc
    assert not accs


def _conv_layer_kernel(x_ref, p_ref, ng_ref, win_ref, wout_ref, cw_ref, cb_ref,
                       lng_ref, lnb_ref, png_ref, gw_ref, pw_ref, fg_ref, o_ref,
                       h_ref, g_ref, pk_ref, c_ref, y_ref, s1_ref, s2_ref, mu_ref, rstd_ref,
                       *, tiles_per_seq, final):
    e = y_ref.shape[1]
    n_chunks = e // CN
    n_blocks = e // LANES
    per_chunk = CN // LANES

    first = pl.program_id(0) % tiles_per_seq == 0

    @pl.when(first)
    def _():
        g_ref[:, 0:HALO, :] = jnp.zeros((n_blocks, HALO, LANES), F32)

    @pl.when(jnp.logical_not(first))
    def _():
        g_ref[:, 0:HALO, :] = g_ref[:, TM:TM + HALO, :]

    g_ref[:, HALO + TM:HALO + TM + G_PAD, :] = jnp.zeros((n_blocks, G_PAD, LANES), F32)

    _norm_rows(x_ref, ng_ref, h_ref)

    s1_ref[...] = jnp.zeros((TM, LANES), F32)
    s2_ref[...] = jnp.zeros((TM, LANES), F32)

    def glu_chunk(j):
        h = h_ref[...]
        ab = _dot(h, win_ref[j, :, 0:2 * CN])
        g = ab[:, :CN] * _sigmoid(ab[:, CN:])
        for l in range(per_chunk):
            g_ref[j * per_chunk + l, HALO:HALO + TM, :] = g[:, l * LANES:(l + 1) * LANES]

    def conv_chunk(j):
        for l in range(per_chunk):
            _conv_block(g_ref.at[j * per_chunk + l], pk_ref.at[l], cw_ref, cb_ref, c_ref, s1_ref, s2_ref,
                        j * per_chunk + l)

    glu_chunk(0)

    def glu_conv(j, _):
        conv_chunk(j - 1)
        glu_chunk(j)
        return 0

    lax.fori_loop(1, n_chunks, glu_conv, 0)

    def last_conv(l, _):
        cb = (n_chunks - 1) * per_chunk + l
        _conv_block(g_ref.at[cb], pk_ref.at[0], cw_ref, cb_ref, c_ref, s1_ref, s2_ref, cb, split=TM // CONV_ROWS)
        return 0

    lax.fori_loop(0, per_chunk, last_conv, 0)

    _finish_stats(s1_ref, s2_ref, mu_ref, rstd_ref, e)

    for j in range(n_chunks):
        z = _dot(h_ref[...], win_ref[j, :, 2 * CN:3 * CN])
        lng = lng_ref[j]
        lnb = lnb_ref[j]
        for l in range(per_chunk):
            sl = slice(l * LANES, (l + 1) * LANES)
            yn = (c_ref[j * per_chunk + l] - mu_ref[...]) * rstd_ref[...] * lng[:, sl] + lnb[:, sl]
            y = _silu(yn) * _silu(z[:, sl])
            col = j * CN + l * LANES
            y_ref[:, col:col + LANES] = y.astype(BF16)

    _tail(x_ref, p_ref, y_ref, wout_ref, png_ref, gw_ref, pw_ref, fg_ref, o_ref, final)


def _sgu_layer_kernel(x_ref, p_ref, ng_ref, win_ref, wout_ref, sw_ref, sb_ref,
                      lng_ref, lnb_ref, png_ref, gw_ref, pw_ref, fg_ref, o_ref,
                      h_ref, u_ref, c_ref, y_ref, s1_ref, s2_ref, mu_ref, rstd_ref,
                      *, final):
    e = y_ref.shape[1]
    n_chunks = e // CN
    per_chunk = CN // LANES

    _norm_rows(x_ref, ng_ref, h_ref)

    s1_ref[...] = jnp.zeros((TM, LANES), F32)
    s2_ref[...] = jnp.zeros((TM, LANES), F32)

    def gelu_chunk(j, _):
        h = h_ref[...]
        u_ref[j] = _gelu(_dot(h, _w_in_chunk(win_ref, 0, j)))
        gb = _gelu(_dot(h, _w_in_chunk(win_ref, 1, j)))
        c_ref[j] = gb
        s1 = s1_ref[...]
        s2 = s2_ref[...]
        for l in range(per_chunk):
            blk = gb[:, l * LANES:(l + 1) * LANES]
            s1 = s1 + blk
            s2 = s2 + blk * blk
        s1_ref[...] = s1
        s2_ref[...] = s2
        return 0

    lax.fori_loop(0, n_chunks, gelu_chunk, 0, unroll=8)

    _finish_stats(s1_ref, s2_ref, mu_ref, rstd_ref, e)

    row = lax.broadcasted_iota(jnp.int32, (SGU_CHUNK, SGU_CHUNK), 0)
    col = lax.broadcasted_iota(jnp.int32, (SGU_CHUNK, SGU_CHUNK), 1)
    causal = col <= row

    for j in range(n_chunks):
        z = _dot(h_ref[...], _w_in_chunk(win_ref, 2, j))
        mu = jnp.concatenate([mu_ref[...]] * per_chunk, axis=1)
        rstd = jnp.concatenate([rstd_ref[...]] * per_chunk, axis=1)
        v = ((c_ref[j] - mu) * rstd * lng_ref[j] + lnb_ref[j]).astype(BF16)
        w = jnp.where(causal, sw_ref[j], 0.0).astype(BF16)
        bias = jnp.concatenate([sb_ref[j]] * per_chunk, axis=1)
        for rc in range(TM // SGU_CHUNK):
            rows = slice(rc * SGU_CHUNK, (rc + 1) * SGU_CHUNK)
            mixed = _dot(w, v[rows]) + bias
            y = u_ref[j, rows, :] * mixed * _silu(z[rows])
            y_ref[rows, j * CN:(j + 1) * CN] = y.astype(BF16)

    _tail(x_ref, p_ref, y_ref, wout_ref, png_ref, gw_ref, pw_ref, fg_ref, o_ref, final)


def _resident(shape):
    zeros = (0,) * len(shape)
    return pl.BlockSpec(shape, lambda i: zeros, pipeline_mode=pl.Buffered(1))


def _resident_layer(shape, layer):
    zeros = (0,) * (len(shape) - 1)
    return pl.BlockSpec((None,) + tuple(shape[1:]), lambda i: (layer,) + zeros, pipeline_mode=pl.Buffered(1))


def _layer_call(kernel_body, name, scratch, layer, x2, p3, ng, win, wout, m0, m1, lng, lnb, png, gw, pw, fg):
    n, d = x2.shape
    ple = p3.shape[2]
    p_spec = pl.BlockSpec((None, TM, ple), lambda i: (layer, i, 0))
    row_spec = lambda width: pl.BlockSpec((TM, width), lambda i: (i, 0))
    params = (ng, win, wout, m0, m1, lng, lnb, png, gw, pw, fg)
    stacked = (1, 2, 8, 9)
    param_specs = [_resident_layer(a.shape, layer // N_MIXERS if k == 1 else layer) if k in stacked
                   else _resident(a.shape) for k, a in enumerate(params)]
    return pl.pallas_call(
        kernel_body,
        name=name,
        grid=(n // TM,),
        in_specs=[row_spec(d), p_spec] + param_specs,
        out_specs=row_spec(d),
        out_shape=jax.ShapeDtypeStruct((n, d), F32),
        scratch_shapes=scratch,
        compiler_params=pltpu.CompilerParams(
            dimension_semantics=("arbitrary",),
            vmem_limit_bytes=VMEM_LIMIT_BYTES),
    )(x2, p3, *params)


def kernel(x, p, norm_g, w_in, w_out, conv_w, conv_b, conv_ln_g, conv_ln_b,
           sgu_ln_g, sgu_ln_b, sgu_w, sgu_b, pl_norm_g, pl_gate_w, pl_proj_w, final_g):
    bsz, seq, d = x.shape
    depth = w_in.shape[0]
    e = w_out.shape[1]
    ple = p.shape[-1]
    n = bsz * seq
    assert seq % TM == 0 and TM % SGU_CHUNK == 0 and e % CN == 0
    assert CN == e // sgu_w.shape[1] and sgu_w.shape[2] == SGU_CHUNK
    assert conv_w.shape[1] == CONV_K and HALO >= CONV_K - 1
    n_chunks = e // CN
    n_blocks = e // LANES

    stat = pltpu.VMEM((TM, LANES), F32)
    common = [stat, stat, stat, stat]
    conv_scratch = [pltpu.VMEM((TM, d), BF16),
                    pltpu.VMEM((n_blocks, HALO + TM + G_PAD, LANES), F32),
                    pltpu.VMEM((CN // LANES, 2, (HALO + TM) // 2, LANES), jnp.uint32),
                    pltpu.VMEM((n_blocks, TM, LANES), F32),
                    pltpu.VMEM((TM, e), BF16)] + common
    sgu_scratch = [pltpu.VMEM((TM, d), BF16),
                   pltpu.VMEM((n_chunks, TM, CN), F32),
                   pltpu.VMEM((n_chunks, TM, CN), F32),
                   pltpu.VMEM((TM, e), BF16)] + common

    x2 = x.reshape(n, d)
    fg = final_g.reshape(1, d)
    win5 = w_in.astype(BF16).reshape(depth, d, 3, n_chunks, CN)
    win_conv = win5[0::N_MIXERS].transpose(0, 3, 1, 2, 4).reshape(-1, n_chunks, d, 3 * CN)
    win_sgu = win5[1::N_MIXERS].transpose(0, 2, 3, 1, 4)
    wout = w_out.astype(BF16)
    gw = pl_gate_w.astype(BF16)
    pw = pl_proj_w.astype(BF16)
    for i in range(depth):
        j = i // 2
        final = i == depth - 1
        args = dict(
            layer=i, x2=x2, p3=p.reshape(depth, n, ple), ng=norm_g[i].reshape(1, d),
            win=win_conv if i % N_MIXERS == 0 else win_sgu,
            wout=wout, png=pl_norm_g[i].reshape(1, d), gw=gw, pw=pw, fg=fg)
        if i % 2 == 0:
            body = functools.partial(_conv_layer_kernel, tiles_per_seq=seq // TM, final=final)
            x2 = _layer_call(
                body, f"conv_layer{i}", conv_scratch,
                m0=jnp.broadcast_to(
                    conv_w[j].astype(BF16).reshape(CONV_K, n_blocks, 1, LANES).transpose(1, 0, 2, 3),
                    (n_blocks, CONV_K, CONV_ROWS, LANES)),
                m1=conv_b[j].reshape(n_blocks, 1, LANES),
                lng=conv_ln_g[j].reshape(n_chunks, 1, CN),
                lnb=conv_ln_b[j].reshape(n_chunks, 1, CN), **args)
        else:
            body = functools.partial(_sgu_layer_kernel, final=final)
            x2 = _layer_call(
                body, f"sgu_layer{i}", sgu_scratch,
                m0=sgu_w[j],
                m1=jnp.broadcast_to(sgu_b[j][:, :, None], (e // CN, SGU_CHUNK, LANES)),
                lng=sgu_ln_g[j].reshape(n_chunks, 1, CN),
                lnb=sgu_ln_b[j].reshape(n_chunks, 1, CN), **args)
    return x2.reshape(bsz, seq, d)
```
